```python
import jax, jax.numpy as jnp
from jax import lax
import numpy as np

D_MODEL = 4096
BATCH = 2
SEQ = 4096
DEPTH = 2

N_MIXERS = 2
N_ATTN_LAYERS = (DEPTH + 1) // 2
N_RNN_LAYERS = DEPTH // 2
DA_HEAD_DIM = 128
DA_HEADS = D_MODEL // (2 * DA_HEAD_DIM)
ROT_DIM = DA_HEAD_DIM // 4
ROPE_THETA = 500000.0
Q_BLOCK = 128
LAMBDA_STD = 0.1
LRU_WIDTH = D_MODEL
LRU_BLOCKS = 16
LRU_BLOCK_W = LRU_WIDTH // LRU_BLOCKS
CONV_WIDTH = 4
CONV_LEFT = 2
LRU_C = 8.0
MEM_TOKENS = 256
X_HEADS = 4
X_HEAD_DIM = 128
X_WIDTH = X_HEADS * X_HEAD_DIM
D_FF = 4 * D_MODEL
EPS = 1e-6

kernel_name = 'hybrid_diffattn_rglru_encoder'


def rmsnorm(x, g):
    xf = x.astype(jnp.float32)
    y = xf * lax.rsqrt(jnp.mean(xf * xf, axis=-1, keepdims=True) + EPS)
    return (y * g.astype(jnp.float32)).astype(x.dtype)


def lambda_init(layer_idx):
    return 0.8 - 0.6 * float(np.exp(-0.3 * layer_idx))


def apply_partial_rope(t, cos, sin):
    half = ROT_DIM // 2
    tf = t.astype(jnp.float32)
    r1, r2, rest = tf[..., :half], tf[..., half:ROT_DIM], tf[..., ROT_DIM:]
    out = jnp.concatenate([r1 * cos - r2 * sin, r2 * cos + r1 * sin, rest], axis=-1)
    return out.astype(t.dtype)


def diff_attention(xn, w_qkv, lq1, lk1, lq2, lk2, subln_g, w_o, cos, sin, lam_init):
    B, S, _ = xn.shape
    q, k, v = jnp.split(xn @ w_qkv, 3, axis=-1)
    q = apply_partial_rope(q.reshape(B, S, DA_HEADS, 2, DA_HEAD_DIM), cos, sin) * (DA_HEAD_DIM ** -0.5)
    k = apply_partial_rope(k.reshape(B, S, DA_HEADS, 2, DA_HEAD_DIM), cos, sin)
    v = v.reshape(B, S, DA_HEADS, 2 * DA_HEAD_DIM)
    f32 = jnp.float32
    lam = (jnp.exp(jnp.sum(lq1.astype(f32) * lk1.astype(f32)))
           - jnp.exp(jnp.sum(lq2.astype(f32) * lk2.astype(f32))) + lam_init)
    nb = S // Q_BLOCK
    qb = q.reshape(B, nb, Q_BLOCK, DA_HEADS, 2, DA_HEAD_DIM).transpose(1, 0, 3, 4, 2, 5)
    kt = k.transpose(0, 2, 3, 1, 4)
    vt = v.transpose(0, 2, 1, 3)

    def block(qblk):
        s = jnp.einsum('bhmqd,bhmkd->bhmqk', qblk, kt).astype(f32)
        p = jax.nn.softmax(s, axis=-1)
        w = p[:, :, 0] - lam * p[:, :, 1]
        return jnp.einsum('bhqk,bhkv->bhqv', w.astype(vt.dtype), vt)

    o = lax.map(block, qb)
    o = o.transpose(1, 0, 3, 2, 4).reshape(B, S, DA_HEADS, 2 * DA_HEAD_DIM)
    o = rmsnorm(o, subln_g) * (1.0 - lam_init)
    return o.reshape(B, S, DA_HEADS * 2 * DA_HEAD_DIM) @ w_o


def _linear_combine(c1, c2):
    a1, b1 = c1
    a2, b2 = c2
    return a1 * a2, a2 * b1 + b2


def _block_diag(u, w, b):
    B, S, _ = u.shape
    ub = u.reshape(B, S, LRU_BLOCKS, LRU_BLOCK_W)
    return (jnp.einsum('bsnc,ncd->bsnd', ub, w) + b).reshape(B, S, LRU_WIDTH)


def rglru_scan(u, w_a, b_a, w_i, b_i, lam, reverse):
    f32 = jnp.float32
    r = jax.nn.sigmoid(_block_diag(u, w_a, b_a).astype(f32))
    ig = jax.nn.sigmoid(_block_diag(u, w_i, b_i).astype(f32))
    log_a = -LRU_C * r * jax.nn.softplus(-lam.astype(f32))
    a = jnp.exp(log_a)
    inp = jnp.sqrt(-jnp.expm1(2.0 * log_a)) * (ig * u.astype(f32))
    _, h = lax.associative_scan(_linear_combine, (a, inp), axis=1, reverse=reverse)
    return h.astype(u.dtype)


def rglru_block(xn, w_in, conv_w, conv_b, wa_f, ba_f, wi_f, bi_f, lam_f,
                wa_b, ba_b, wi_b, bi_b, lam_b, w_out):
    S = xn.shape[1]
    u, gate = jnp.split(xn @ w_in, 2, axis=-1)
    up = jnp.pad(u, ((0, 0), (CONV_LEFT, CONV_WIDTH - 1 - CONV_LEFT), (0, 0)))
    uc = sum((up[:, t:t + S] * conv_w[t] for t in range(CONV_WIDTH)), conv_b)
    h = (rglru_scan(uc, wa_f, ba_f, wi_f, bi_f, lam_f, False)
         + rglru_scan(uc, wa_b, ba_b, wi_b, bi_b, lam_b, True))
    return (h * jax.nn.gelu(gate, approximate=True)) @ w_out


def memory_xattn(xn, mem, mem_g, w_q, w_kv, w_o):
    B, S, _ = xn.shape
    q = (xn @ w_q).reshape(B, S, X_HEADS, X_HEAD_DIM) * (X_HEAD_DIM ** -0.5)
    k, v = jnp.split(rmsnorm(mem, mem_g) @ w_kv, 2, axis=-1)
    k = k.reshape(B, -1, X_HEADS, X_HEAD_DIM)
    v = v.reshape(B, -1, X_HEADS, X_HEAD_DIM)
    p = jax.nn.softmax(jnp.einsum('bqhd,bkhd->bhqk', q, k).astype(jnp.float32), axis=-1)
    o = jnp.einsum('bhqk,bkhd->bqhd', p.astype(v.dtype), v).reshape(B, S, X_WIDTH)
    return o @ w_o


def sq_relu_mlp(xn, w1, w2):
    return jnp.square(jax.nn.relu(xn @ w1)) @ w2


def setup_inputs(seed: int = 0) -> dict:
    key = jax.random.key(seed)
    ks = iter(jax.random.split(key, 64))
    f32 = jnp.float32

    def nrm(shape, scale):
        return jax.random.normal(next(ks), shape, f32) * scale

    def gain(shape):
        return 1.0 + nrm(shape, 0.02)

    def lru_lambda(n):
        a_c = jax.random.uniform(next(ks), (n, LRU_WIDTH), f32, 0.9, 0.999)
        s = a_c ** (1.0 / LRU_C)
        return jnp.log(s) - jnp.log1p(-s)

    NA, NR = N_ATTN_LAYERS, N_RNN_LAYERS
    x = jax.random.normal(next(ks), (BATCH, SEQ, D_MODEL), f32)
    mem = jax.random.normal(next(ks), (BATCH, MEM_TOKENS, D_MODEL), f32)
    offsets = jax.random.randint(next(ks), (BATCH, 1), 0, 1024, jnp.int32)
    positions = jnp.arange(SEQ, dtype=jnp.int32)[None, :] + offsets
    d = D_MODEL ** -0.5
    return {
        'x': x, 'mem': mem, 'positions': positions,
        'attn_norm_g': gain((NA, D_MODEL)),
        'attn_w_qkv': nrm((NA, D_MODEL, 3 * D_MODEL), d),
        'attn_lambda_q1': nrm((NA, DA_HEAD_DIM), LAMBDA_STD),
        'attn_lambda_k1': nrm((NA, DA_HEAD_DIM), LAMBDA_STD),
        'attn_lambda_q2': nrm((NA, DA_HEAD_DIM), LAMBDA_STD),
        'attn_lambda_k2': nrm((NA, DA_HEAD_DIM), LAMBDA_STD),
        'attn_subln_g': gain((NA, 2 * DA_HEAD_DIM)),
        'attn_w_o': nrm((NA, D_MODEL, D_MODEL), d),
        'rnn_norm_g': gain((NR, D_MODEL)),
        'rnn_w_in': nrm((NR, D_MODEL, 2 * LRU_WIDTH), d),
        'rnn_conv_w': nrm((NR, CONV_WIDTH, LRU_WIDTH), CONV_WIDTH ** -0.5),
        'rnn_conv_b': nrm((NR, LRU_WIDTH), 0.01),
        'rnn_wa_f': nrm((NR, LRU_BLOCKS, LRU_BLOCK_W, LRU_BLOCK_W), LRU_BLOCK_W ** -0.5),
        'rnn_ba_f': nrm((NR, LRU_BLOCKS, LRU_BLOCK_W), 0.01),
        'rnn_wi_f': nrm((NR, LRU_BLOCKS, LRU_BLOCK_W, LRU_BLOCK_W), LRU_BLOCK_W ** -0.5),
        'rnn_bi_f': nrm((NR, LRU_BLOCKS, LRU_BLOCK_W), 0.01),
        'rnn_lam_f': lru_lambda(NR),
        'rnn_wa_b': nrm((NR, LRU_BLOCKS, LRU_BLOCK_W, LRU_BLOCK_W), LRU_BLOCK_W ** -0.5),
        'rnn_ba_b': nrm((NR, LRU_BLOCKS, LRU_BLOCK_W), 0.01),
        'rnn_wi_b': nrm((NR, LRU_BLOCKS, LRU_BLOCK_W, LRU_BLOCK_W), LRU_BLOCK_W ** -0.5),
        'rnn_bi_b': nrm((NR, LRU_BLOCKS, LRU_BLOCK_W), 0.01),
        'rnn_lam_b': lru_lambda(NR),
        'rnn_w_out': nrm((NR, LRU_WIDTH, D_MODEL), LRU_WIDTH ** -0.5),
        'xattn_norm_g': gain((DEPTH, D_MODEL)),
        'xattn_mem_g': gain((DEPTH, D_MODEL)),
        'xattn_w_q': nrm((DEPTH, D_MODEL, X_WIDTH), d),
        'xattn_w_kv': nrm((DEPTH, D_MODEL, 2 * X_WIDTH), d),
        'xattn_w_o': nrm((DEPTH, X_WIDTH, D_MODEL), X_WIDTH ** -0.5),
        'mlp_norm_g': gain((DEPTH, D_MODEL)),
        'mlp_w1': nrm((DEPTH, D_MODEL, D_FF), d),
        'mlp_w2': nrm((DEPTH, D_FF, D_MODEL), D_FF ** -0.5),
        'final_g': gain((D_MODEL,)),
    }


def reference(x, mem, positions,
              attn_norm_g, attn_w_qkv, attn_lambda_q1, attn_lambda_k1, attn_lambda_q2,
              attn_lambda_k2, attn_subln_g, attn_w_o,
              rnn_norm_g, rnn_w_in, rnn_conv_w, rnn_conv_b,
              rnn_wa_f, rnn_ba_f, rnn_wi_f, rnn_bi_f, rnn_lam_f,
              rnn_wa_b, rnn_ba_b, rnn_wi_b, rnn_bi_b, rnn_lam_b, rnn_w_out,
              xattn_norm_g, xattn_mem_g, xattn_w_q, xattn_w_kv, xattn_w_o,
              mlp_norm_g, mlp_w1, mlp_w2, final_g):
    inv_freq = ROPE_THETA ** (-jnp.arange(0, ROT_DIM, 2, dtype=jnp.float32) / ROT_DIM)
    ang = positions.astype(jnp.float32)[..., None] * inv_freq
    cos = jnp.cos(ang)[:, :, None, None, :]
    sin = jnp.sin(ang)[:, :, None, None, :]
    h = x
    for i in range(DEPTH):
        j = i // N_MIXERS
        if i % N_MIXERS == 0:
            h = h + diff_attention(rmsnorm(h, attn_norm_g[j]), attn_w_qkv[j],
                                   attn_lambda_q1[j], attn_lambda_k1[j],
                                   attn_lambda_q2[j], attn_lambda_k2[j],
                                   attn_subln_g[j], attn_w_o[j], cos, sin, lambda_init(i))
        else:
            h = h + rglru_block(rmsnorm(h, rnn_norm_g[j]), rnn_w_in[j], rnn_conv_w[j], rnn_conv_b[j],
                                rnn_wa_f[j], rnn_ba_f[j], rnn_wi_f[j], rnn_bi_f[j], rnn_lam_f[j],
                                rnn_wa_b[j], rnn_ba_b[j], rnn_wi_b[j], rnn_bi_b[j], rnn_lam_b[j],
                                rnn_w_out[j])
        h = h + memory_xattn(rmsnorm(h, xattn_norm_g[i]), mem, xattn_mem_g[i],
                             xattn_w_q[i], xattn_w_kv[i], xattn_w_o[i])
        h = h + sq_relu_mlp(rmsnorm(h, mlp_norm_g[i]), mlp_w1[i], mlp_w2[i])
    return rmsnorm(h, final_g)
```

```python
import functools
import math

import jax
import jax.numpy as jnp
from jax import lax
from jax.experimental import pallas as pl
from jax.experimental.pallas import tpu as pltpu

F32 = jnp.float32
BF16 = jnp.bfloat16

EPS = 1e-6
ROPE_THETA = 500000.0
DA_HEAD_DIM = 128
ROT_DIM = DA_HEAD_DIM // 4
LRU_BLOCK_W = 256
LRU_C = 8.0
CONV_WIDTH = 4
CONV_LEFT = 2
X_HEADS = 4
X_HEAD_DIM = 128
N_MIXERS = 2

LANES = 128
SUBLANES = 8
VMEM_LIMIT_BYTES = 56 * 1024 * 1024


def _params(*semantics):
    return pltpu.CompilerParams(dimension_semantics=semantics,
                                vmem_limit_bytes=VMEM_LIMIT_BYTES)


def _cast_kernel(w_ref, o_ref):
    o_ref[...] = w_ref[...].astype(o_ref.dtype)


def cast_weight(w3, layer, bk=512, bn=2048):
    _, K, N = w3.shape
    bk, bn = min(bk, K), min(bn, N)
    return pl.pallas_call(
        _cast_kernel,
        grid=(K // bk, N // bn),
        in_specs=[pl.BlockSpec((None, bk, bn), lambda i, j: (layer, i, j))],
        out_specs=pl.BlockSpec((bk, bn), lambda i, j: (i, j)),
        out_shape=jax.ShapeDtypeStruct((K, N), BF16),
        compiler_params=_params("parallel", "parallel"),
        name="cast_weight",
    )(w3)


def _rope_table_kernel(pos_ref, invf_ref, cos_ref, sa_ref, sb_ref):
    ang = pos_ref[...].astype(F32) * invf_ref[...]
    lane = lax.broadcasted_iota(jnp.int32, ang.shape, 1)
    half = ROT_DIM // 2
    c, s = jnp.cos(ang), jnp.sin(ang)
    cos_ref[...] = jnp.where(lane < ROT_DIM, c, 1.0)
    sa_ref[...] = jnp.where(lane < half, -s, 0.0)
    sb_ref[...] = jnp.where((lane >= half) & (lane < ROT_DIM), s, 0.0)


def rope_tables(positions, tm=1024):
    T = positions.size
    tm = min(tm, T)
    half = ROT_DIM // 2
    inv_freq = ROPE_THETA ** (-jnp.arange(0, ROT_DIM, 2, dtype=F32) / ROT_DIM)
    invf = jnp.tile(inv_freq, LANES // half)[None, :]
    tab = jax.ShapeDtypeStruct((T, LANES), F32)
    spec = pl.BlockSpec((tm, LANES), lambda i: (i, 0))
    return pl.pallas_call(
        _rope_table_kernel,
        grid=(T // tm,),
        in_specs=[pl.BlockSpec((tm, 1), lambda i: (i, 0)),
                  pl.BlockSpec((1, LANES), lambda i: (0, 0))],
        out_specs=[spec, spec, spec],
        out_shape=[tab, tab, tab],
        compiler_params=_params("parallel"),
        name="rope_tables",
    )(positions.reshape(T, 1), invf)


def _rmsnorm_kernel(x_ref, g_ref, o_ref):
    x = x_ref[...].astype(F32)
    ms = jnp.mean(x * x, axis=-1, keepdims=True)
    o_ref[...] = (x * lax.rsqrt(ms + EPS) * g_ref[...]).astype(o_ref.dtype)


def rmsnorm(x, g2, layer, out_dtype, tm=256):
    T, D = x.shape
    tm = min(tm, T)
    g3 = g2.reshape(g2.shape[0], 1, D)
    return pl.pallas_call(
        _rmsnorm_kernel,
        grid=(T // tm,),
        in_specs=[pl.BlockSpec((tm, D), lambda i: (i, 0)),
                  pl.BlockSpec((None, 1, D), lambda i: (layer, 0, 0))],
        out_specs=pl.BlockSpec((tm, D), lambda i: (i, 0)),
        out_shape=jax.ShapeDtypeStruct((T, D), out_dtype),
        compiler_params=_params("parallel"),
        name="rmsnorm",
    )(x, g3)


def _rope_cols(t, cos, sa, sb):
    half = ROT_DIM // 2
    return (t * cos + pltpu.roll(t, LANES - half, 1) * sa + pltpu.roll(t, half, 1) * sb)


def _mm_kernel(*refs, nk, relu2, scale, has_res, rope_tiles):
    it = iter(refs)
    a_ref, w_ref = next(it), next(it)
    res_ref = next(it) if has_res else None
    if rope_tiles:
        cos_ref, sa_ref, sb_ref = next(it), next(it), next(it)
    o_ref = next(it)
    acc_ref = next(it) if nk > 1 else None

    part = jnp.dot(a_ref[...], w_ref[...], preferred_element_type=F32)

    def finish(acc):
        if rope_tiles:
            j = pl.program_id(1)
            tn = acc.shape[1]

            def roped(qscale):
                for c in range(tn // LANES):
                    t = _rope_cols(acc[:, c * LANES:(c + 1) * LANES],
                                   cos_ref[...], sa_ref[...], sb_ref[...])
                    if qscale is not None:
                        t = t * qscale
                    o_ref[:, c * LANES:(c + 1) * LANES] = t.astype(o_ref.dtype)

            @pl.when(j < rope_tiles)
            def _():
                roped(scale)

            @pl.when((j >= rope_tiles) & (j < 2 * rope_tiles))
            def _():
                roped(None)

            @pl.when(j >= 2 * rope_tiles)
            def _():
                o_ref[...] = acc.astype(o_ref.dtype)
            return
        if relu2:
            r = jnp.maximum(acc, 0.0)
            acc = r * r
        if scale is not None:
            acc = acc * scale
        if has_res:
            acc = acc + res_ref[...]
        o_ref[...] = acc.astype(o_ref.dtype)

    if nk == 1:
        finish(part)
        return

    k = pl.program_id(2)

    @pl.when(k == 0)
    def _():
        acc_ref[...] = part

    @pl.when(k > 0)
    def _():
        acc_ref[...] += part

    @pl.when(k == nk - 1)
    def _():
        finish(acc_ref[...])


def matmul(a, w, *, out_dtype, tm=1024, tn=2048, tk=512, relu2=False, scale=None,
           res=None, rope=None, rope_width=None):
    M, K = a.shape
    _, N = w.shape
    tm, tn, tk = min(tm, M), min(tn, N), min(tk, K)
    if rope is not None:
        tn = min(tn, rope_width)
    nk = K // tk
    in_specs = [pl.BlockSpec((tm, tk), lambda i, j, k: (i, k)),
                pl.BlockSpec((tk, tn), lambda i, j, k: (k, j))]
    args = [a, w]
    if res is not None:
        in_specs.append(pl.BlockSpec((tm, tn), lambda i, j, k: (i, j)))
        args.append(res)
    rope_tiles = 0
    if rope is not None:
        assert rope_width % tn == 0
        rope_tiles = rope_width // tn
        in_specs += [pl.BlockSpec((tm, LANES), lambda i, j, k: (i, 0))] * 3
        args += list(rope)
    kern = functools.partial(_mm_kernel, nk=nk, relu2=relu2, scale=scale,
                             has_res=res is not None, rope_tiles=rope_tiles)
    return pl.pallas_call(
        kern,
        grid=(M // tm, N // tn, nk),
        in_specs=in_specs,
        out_specs=pl.BlockSpec((tm, tn), lambda i, j, k: (i, j)),
        out_shape=jax.ShapeDtypeStruct((M, N), out_dtype),
        scratch_shapes=[pltpu.VMEM((tm, tn), F32)] if nk > 1 else [],
        compiler_params=_params("parallel", "parallel", "arbitrary"),
        name="matmul",
    )(*args)


def _dattn_kernel(lam_ref, q_ref, k_ref, v_ref, g_ref, o_ref, s_ref, *, kc, lam_init):
    tq = q_ref.shape[0]
    nkc = k_ref.shape[0] // kc
    d = DA_HEAD_DIM
    lv = lam_ref[...]
    lam = (jnp.exp(jnp.sum(lv[0:1] * lv[1:2], axis=1, keepdims=True))
           - jnp.exp(jnp.sum(lv[2:3] * lv[3:4], axis=1, keepdims=True)) + lam_init)

    heads = []
    for m in range(2):
        qm = q_ref[:, m * d:(m + 1) * d]

        def scores(c, mp):
            kk = k_ref[pl.ds(pl.multiple_of(c * kc, kc), kc), m * d:(m + 1) * d]
            s = lax.dot_general(qm, kk, (((1,), (1,)), ((), ())), preferred_element_type=F32)
            s_ref[c] = s
            for t in range(kc // LANES):
                mp = jnp.maximum(mp, s[:, t * LANES:(t + 1) * LANES])
            return mp

        mp = lax.fori_loop(0, nkc, scores, jnp.full((tq, LANES), -jnp.inf, F32))
        mrow = jnp.max(mp, axis=1, keepdims=True)

        def weighted(c, carry):
            lp, acc = carry
            p = jnp.exp(s_ref[c] - mrow)
            for t in range(kc // LANES):
                lp = lp + p[:, t * LANES:(t + 1) * LANES]
            vv = v_ref[pl.ds(pl.multiple_of(c * kc, kc), kc), :]
            acc = acc + jnp.dot(p.astype(BF16), vv, preferred_element_type=F32)
            return lp, acc

        lp, acc = lax.fori_loop(0, nkc, weighted,
                                (jnp.zeros((tq, LANES), F32), jnp.zeros((tq, 2 * d), F32)))
        heads.append(acc / jnp.sum(lp, axis=1, keepdims=True))

    o = heads[0] - lam * heads[1]
    ms = jnp.mean(o * o, axis=1, keepdims=True)
    o = o * lax.rsqrt(ms + EPS) * g_ref[...] * (1.0 - lam_init)
    o_ref[...] = o.astype(o_ref.dtype)


def diff_attention(qkv, lam4, subln_g, layer, *, B, S, D, lam_init, tq=256, kc=512):
    T = B * S
    hw = 2 * DA_HEAD_DIM
    H = D // hw
    nq = S // tq
    g3 = subln_g.reshape(subln_g.shape[0], 1, hw)
    kern = functools.partial(_dattn_kernel, kc=kc, lam_init=lam_init)
    return pl.pallas_call(
        kern,
        grid=(B, H, nq),
        in_specs=[pl.BlockSpec((None, 4, DA_HEAD_DIM), lambda b, h, i: (layer, 0, 0)),
                  pl.BlockSpec((tq, hw), lambda b, h, i: (b * nq + i, h)),
                  pl.BlockSpec((S, hw), lambda b, h, i: (b, H + h)),
                  pl.BlockSpec((S, hw), lambda b, h, i: (b, 2 * H + h)),
                  pl.BlockSpec((None, 1, hw), lambda b, h, i: (layer, 0, 0))],
        out_specs=pl.BlockSpec((tq, hw), lambda b, h, i: (b * nq + i, h)),
        out_shape=jax.ShapeDtypeStruct((T, D), BF16),
        scratch_shapes=[pltpu.VMEM((S // kc, tq, kc), F32)],
        compiler_params=_params("parallel", "parallel", "arbitrary"),
        name="diff_attention",
    )(lam4, qkv, qkv, qkv, g3)


def _xattn_kernel(q_ref, k_ref, v_ref, o_ref):
    hd = X_HEAD_DIM
    for h in range(X_HEADS):
        q = q_ref[:, h * hd:(h + 1) * hd]
        k = k_ref[:, h * hd:(h + 1) * hd]
        v = v_ref[:, h * hd:(h + 1) * hd]
        s = lax.dot_general(q, k, (((1,), (1,)), ((), ())), preferred_element_type=F32)
        p = jnp.exp(s - jnp.max(s, axis=1, keepdims=True))
        p = p / jnp.sum(p, axis=1, keepdims=True)
        o = jnp.dot(p.astype(BF16), v, preferred_element_type=F32)
        o_ref[:, h * hd:(h + 1) * hd] = o.astype(o_ref.dtype)


def xattn_core(q, kv, *, B, S, tm=512):
    T, XW = q.shape
    M = kv.shape[0] // B
    nt = S // tm
    return pl.pallas_call(
        _xattn_kernel,
        grid=(B, nt),
        in_specs=[pl.BlockSpec((tm, XW), lambda b, i: (b * nt + i, 0)),
                  pl.BlockSpec((M, XW), lambda b, i: (b, 0)),
                  pl.BlockSpec((M, XW), lambda b, i: (b, 1))],
        out_specs=pl.BlockSpec((tm, XW), lambda b, i: (b * nt + i, 0)),
        out_shape=jax.ShapeDtypeStruct((T, XW), BF16),
        compiler_params=_params("parallel", "parallel"),
        name="xattn_core",
    )(q, kv, kv)


def _softplus(x):
    return jnp.maximum(x, 0.0) + jnp.log1p(jnp.exp(-jnp.abs(x)))


def _sigmoid(x):
    return 1.0 / (1.0 + jnp.exp(-x))


def _lru_gate_kernel(u_ref, up_ref, un_ref, cw_ref, cb_ref,
                     waf_ref, baf_ref, wif_ref, bif_ref, lamf_ref,
                     wab_ref, bab_ref, wib_ref, bib_ref, lamb_ref,
                     af_ref, xf_ref, ab_ref, xb_ref, pad_ref):
    i = pl.program_id(2)
    ni = pl.num_programs(2)
    tc = u_ref.shape[0]
    h = SUBLANES
    u = u_ref[...]
    pad_ref[0:h] = jnp.where(i > 0, up_ref[...], 0.0)
    pad_ref[h:h + tc] = u
    pad_ref[h + tc:2 * h + tc] = jnp.where(i < ni - 1, un_ref[...], 0.0)
    cw = cw_ref[...]
    uc = cb_ref[...]
    for t in range(CONV_WIDTH):
        off = h + t - CONV_LEFT
        uc = uc + pad_ref[off:off + tc] * cw[t:t + 1]
    ub = uc.astype(BF16)

    def gate(w_ref, b_ref):
        z = jnp.dot(ub, w_ref[...].astype(BF16), preferred_element_type=F32) + b_ref[...]
        return _sigmoid(z)

    for wa, ba, wi, bi, lam, a_out, x_out in (
            (waf_ref, baf_ref, wif_ref, bif_ref, lamf_ref, af_ref, xf_ref),
            (wab_ref, bab_ref, wib_ref, bib_ref, lamb_ref, ab_ref, xb_ref)):
        r = gate(wa, ba)
        ig = gate(wi, bi)
        log_a = (-LRU_C) * r * _softplus(-lam[...])
        a = jnp.exp(log_a)
        a_out[...] = a
        x_out[...] = jnp.sqrt(jnp.tanh(-log_a) * (1.0 + a * a)) * (ig * uc)


def lru_gates(ug, layer, conv_w, conv_b, dirs, *, B, S, W, tc=1024):
    T = B * S
    bw = LRU_BLOCK_W
    nb = W // bw
    tc = min(tc, S)
    nt = S // tc
    h = SUBLANES
    last8 = T // h - 1

    def u_map(n, b, i):
        return (b * nt + i, n)

    def prev_map(n, b, i):
        return (jnp.maximum((b * S + i * tc) // h - 1, 0), n)

    def next_map(n, b, i):
        return (jnp.minimum((b * S + (i + 1) * tc) // h, last8), n)

    in_specs = [pl.BlockSpec((tc, bw), u_map),
                pl.BlockSpec((h, bw), prev_map),
                pl.BlockSpec((h, bw), next_map),
                pl.BlockSpec((None, CONV_WIDTH, bw), lambda n, b, i: (layer, 0, n)),
                pl.BlockSpec((None, 1, bw), lambda n, b, i: (layer, 0, n))]
    args = [ug, ug, ug, conv_w, conv_b.reshape(conv_b.shape[0], 1, W)]
    wspec = pl.BlockSpec((None, None, bw, bw), lambda n, b, i: (layer, n, 0, 0))
    vspec = pl.BlockSpec((None, None, 1, bw), lambda n, b, i: (layer, n, 0, 0))
    for wa, ba, wi, bi, lam in dirs:
        L = wa.shape[0]
        in_specs += [wspec, vspec, wspec, vspec, vspec]
        args += [wa, ba.reshape(L, nb, 1, bw), wi, bi.reshape(L, nb, 1, bw),
                 lam.reshape(L, nb, 1, bw)]
    out = jax.ShapeDtypeStruct((T, W), F32)
    ospec = pl.BlockSpec((tc, bw), u_map)
    return pl.pallas_call(
        _lru_gate_kernel,
        grid=(nb, B, nt),
        in_specs=in_specs,
        out_specs=[ospec] * 4,
        out_shape=[out] * 4,
        scratch_shapes=[pltpu.VMEM((tc + 2 * h, bw), F32)],
        compiler_params=_params("parallel", "parallel", "parallel"),
        name="lru_gates",
    )(*args)


def _lru_scan_kernel(af_ref, xf_ref, ab_ref, xb_ref, hf_ref, hb_ref, st_ref, *, unroll):
    c = pl.program_id(2)
    tc = af_ref.shape[0]

    @pl.when(c == 0)
    def _():
        st_ref[...] = jnp.zeros_like(st_ref)

    def step(t, carry):
        hf, hb = carry
        hf = af_ref[t] * hf + xf_ref[t]
        hf_ref[t] = hf
        tb = tc - 1 - t
        hb = ab_ref[tb] * hb + xb_ref[tb]
        hb_ref[tb] = hb
        return hf, hb

    hf, hb = lax.fori_loop(0, tc, step, (st_ref[0], st_ref[1]), unroll=unroll)
    st_ref[0] = hf
    st_ref[1] = hb


def lru_scan(a_f, x_f, a_b, x_b, *, B, S, W, tc=256, sl=16, unroll=8):
    wr = W // LANES
    tc, sl = min(tc, S), min(sl, wr)
    nc = S // tc
    shape4 = (B, S, wr, LANES)
    fwd = pl.BlockSpec((None, tc, sl, LANES), lambda b, s, c: (b, c, s, 0))
    bwd = pl.BlockSpec((None, tc, sl, LANES), lambda b, s, c: (b, nc - 1 - c, s, 0))
    out = jax.ShapeDtypeStruct(shape4, F32)
    hf, hb = pl.pallas_call(
        functools.partial(_lru_scan_kernel, unroll=unroll),
        grid=(B, wr // sl, nc),
        in_specs=[fwd, fwd, bwd, bwd],
        out_specs=[fwd, bwd],
        out_shape=[out, out],
        scratch_shapes=[pltpu.VMEM((2, sl, LANES), F32)],
        compiler_params=_params("parallel", "parallel", "arbitrary"),
        name="lru_scan",
    )(a_f.reshape(shape4), x_f.reshape(shape4), a_b.reshape(shape4), x_b.reshape(shape4))
    return hf.reshape(B * S, W), hb.reshape(B * S, W)


def _lru_combine_kernel(hf_ref, hb_ref, g_ref, o_ref):
    g = g_ref[...]
    gelu = 0.5 * g * (1.0 + jnp.tanh(math.sqrt(2.0 / math.pi) * (g + 0.044715 * (g * g * g))))
    o_ref[...] = ((hf_ref[...] + hb_ref[...]) * gelu).astype(o_ref.dtype)


def lru_combine(hf, hb, ug, *, tm=512, tn=1024):
    T, W = hf.shape
    tm, tn = min(tm, T), min(tn, W)
    spec = pl.BlockSpec((tm, tn), lambda i, j: (i, j))
    return pl.pallas_call(
        _lru_combine_kernel,
        grid=(T // tm, W // tn),
        in_specs=[spec, spec, pl.BlockSpec((tm, tn), lambda i, j: (i, W // tn + j))],
        out_specs=spec,
        out_shape=jax.ShapeDtypeStruct((T, W), BF16),
        compiler_params=_params("parallel", "parallel"),
        name="lru_combine",
    )(hf, hb, ug)


def _lambda_init(layer_idx):
    return 0.8 - 0.6 * math.exp(-0.3 * layer_idx)


def kernel(x, mem, positions, attn_norm_g, attn_w_qkv, attn_lambda_q1, attn_lambda_k1, attn_lambda_q2, attn_lambda_k2, attn_subln_g, attn_w_o, rnn_norm_g, rnn_w_in, rnn_conv_w, rnn_conv_b, rnn_wa_f, rnn_ba_f, rnn_wi_f, rnn_bi_f, rnn_lam_f, rnn_wa_b, rnn_ba_b, rnn_wi_b, rnn_bi_b, rnn_lam_b, rnn_w_out, xattn_norm_g, xattn_mem_g, xattn_w_q, xattn_w_kv, xattn_w_o, mlp_norm_g, mlp_w1, mlp_w2, final_g):
    B, S, D = x.shape
    T = B * S
    depth = xattn_norm_g.shape[0]
    M = mem.shape[1]
    W = rnn_lam_f.shape[-1]

    h = x.reshape(T, D)
    mem2 = mem.reshape(B * M, D)
    rope = rope_tables(positions)
    lam4 = jnp.stack([attn_lambda_q1, attn_lambda_k1, attn_lambda_q2, attn_lambda_k2], axis=1)

    for i in range(depth):
        j = i // N_MIXERS
        if i % N_MIXERS == 0:
            xn = rmsnorm(h, attn_norm_g, j, BF16)
            qkv = matmul(xn, cast_weight(attn_w_qkv, j), out_dtype=BF16,
                         scale=DA_HEAD_DIM ** -0.5, rope=rope, rope_width=D)
            o = diff_attention(qkv, lam4, attn_subln_g, j, B=B, S=S, D=D,
                               lam_init=_lambda_init(i))
            h = matmul(o, cast_weight(attn_w_o, j), out_dtype=F32, res=h)
        else:
            xn = rmsnorm(h, rnn_norm_g, j, BF16)
            ug = matmul(xn, cast_weight(rnn_w_in, j), out_dtype=F32)
            a_f, x_f, a_b, x_b = lru_gates(
                ug, j, rnn_conv_w, rnn_conv_b,
                ((rnn_wa_f, rnn_ba_f, rnn_wi_f, rnn_bi_f, rnn_lam_f),
                 (rnn_wa_b, rnn_ba_b, rnn_wi_b, rnn_bi_b, rnn_lam_b)),
                B=B, S=S, W=W)
            hf, hb = lru_scan(a_f, x_f, a_b, x_b, B=B, S=S, W=W)
            y = lru_combine(hf, hb, ug)
            h = matmul(y, cast_weight(rnn_w_out, j), out_dtype=F32, res=h)

        xn = rmsnorm(h, xattn_norm_g, i, BF16)
        q = matmul(xn, cast_weight(xattn_w_q, i), out_dtype=BF16, scale=X_HEAD_DIM ** -0.5)
        kv = matmul(rmsnorm(mem2, xattn_mem_g, i, BF16), cast_weight(xattn_w_kv, i),
                    out_dtype=BF16)
        o = xattn_core(q, kv, B=B, S=S)
        h = matmul(o, cast_weight(xattn_w_o, i), out_dtype=F32, res=h)

        xn = rmsnorm(h, mlp_norm_g, i, BF16)
        hid = matmul(xn, cast_weight(mlp_w1, i), out_dtype=BF16, relu2=True)
        h = matmul(hid, cast_weight(mlp_w2, i), out_dtype=F32, res=h)

    return rmsnorm(h, final_g.reshape(1, D), 0, F32).reshape(B, S, D)
```

```python
import functools
import math

import jax
import jax.numpy as jnp
from jax import lax
from jax.experimental import pallas as pl
from jax.experimental.pallas import tpu as pltpu

F32 = jnp.float32
BF16 = jnp.bfloat16

EPS = 1e-6
ROPE_THETA = 500000.0
DA_HEAD_DIM = 128
ROT_DIM = DA_HEAD_DIM // 4
LRU_BLOCK_W = 256
LRU_C = 8.0
CONV_WIDTH = 4
CONV_LEFT = 2
X_HEADS = 4
X_HEAD_DIM = 128
N_MIXERS = 2

LANES = 128
SUBLANES = 8
VMEM_LIMIT_BYTES = 56 * 1024 * 1024
MM_TK = 4096


def _params(*semantics):
    return pltpu.CompilerParams(dimension_semantics=semantics,
                                vmem_limit_bytes=VMEM_LIMIT_BYTES)


def _rope_table_kernel(pos_ref, invf_ref, cos_ref, sa_ref, sb_ref):
    ang = pos_ref[...].astype(F32) * invf_ref[...]
    lane = lax.broadcasted_iota(jnp.int32, ang.shape, 1)
    half = ROT_DIM // 2
    c, s = jnp.cos(ang), jnp.sin(ang)
    cos_ref[...] = jnp.where(lane < ROT_DIM, c, 1.0)
    sa_ref[...] = jnp.where(lane < half, -s, 0.0)
    sb_ref[...] = jnp.where((lane >= half) & (lane < ROT_DIM), s, 0.0)


def rope_tables(positions, tm=1024):
    T = positions.size
    tm = min(tm, T)
    half = ROT_DIM // 2
    inv_freq = ROPE_THETA ** (-jnp.arange(0, ROT_DIM, 2, dtype=F32) / ROT_DIM)
    invf = jnp.tile(inv_freq, LANES // half)[None, :]
    tab = jax.ShapeDtypeStruct((T, LANES), F32)
    spec = pl.BlockSpec((tm, LANES), lambda i: (i, 0))
    return pl.pallas_call(
        _rope_table_kernel,
        grid=(T // tm,),
        in_specs=[pl.BlockSpec((tm, 1), lambda i: (i, 0)),
                  pl.BlockSpec((1, LANES), lambda i: (0, 0))],
        out_specs=[spec, spec, spec],
        out_shape=[tab, tab, tab],
        compiler_params=_params("parallel"),
        name="rope_tables",
    )(positions.reshape(T, 1), invf)


def _rmsnorm_kernel(x_ref, g_ref, o_ref):
    x = x_ref[...].astype(F32)
    ms = jnp.mean(x * x, axis=-1, keepdims=True)
    o_ref[...] = (x * lax.rsqrt(ms + EPS) * g_ref[...]).astype(o_ref.dtype)


def rmsnorm(x, g2, layer, out_dtype, tm=256):
    T, D = x.shape
    tm = min(tm, T)
    g3 = g2.reshape(g2.shape[0], 1, D)
    return pl.pallas_call(
        _rmsnorm_kernel,
        grid=(T // tm,),
        in_specs=[pl.BlockSpec((tm, D), lambda i: (i, 0)),
                  pl.BlockSpec((None, 1, D), lambda i: (layer, 0, 0))],
        out_specs=pl.BlockSpec((tm, D), lambda i: (i, 0)),
        out_shape=jax.ShapeDtypeStruct((T, D), out_dtype),
        compiler_params=_params("parallel"),
        name="rmsnorm",
    )(x, g3)


def _rope_cols(t, cos, sa, sb):
    half = ROT_DIM // 2
    return (t * cos + pltpu.roll(t, LANES - half, 1) * sa + pltpu.roll(t, half, 1) * sb)


def _mm_kernel(*refs, relu2, scale, has_res, rope_tiles):
    it = iter(refs)
    a_ref, w_ref = next(it), next(it)
    res_ref = next(it) if has_res else None
    if rope_tiles:
        cos_ref, sa_ref, sb_ref = next(it), next(it), next(it)
    o_ref, wbf_ref = next(it), next(it)

    @pl.when(pl.program_id(1) == 0)
    def _():
        wbf_ref[...] = w_ref[...].astype(BF16)

    acc = jnp.dot(a_ref[...], wbf_ref[...], preferred_element_type=F32)

    if rope_tiles:
        j = pl.program_id(0)
        rotary = j < 2 * rope_tiles
        qs = jnp.where(j < rope_tiles, scale, 1.0).astype(F32)
        cos = jnp.where(rotary, cos_ref[...], 1.0) * qs
        sa = jnp.where(rotary, sa_ref[...], 0.0) * qs
        sb = jnp.where(rotary, sb_ref[...], 0.0) * qs
        for c in range(acc.shape[1] // LANES):
            t = _rope_cols(acc[:, c * LANES:(c + 1) * LANES], cos, sa, sb)
            o_ref[:, c * LANES:(c + 1) * LANES] = t.astype(o_ref.dtype)
        return
    if relu2:
        r = jnp.maximum(acc, 0.0)
        acc = r * r
    if scale is not None:
        acc = acc * scale
    if has_res:
        acc = acc + res_ref[...]
    o_ref[...] = acc.astype(o_ref.dtype)


def matmul(a, w3, layer, *, out_dtype, kslab=0, tm=1024, tn=512, tk=MM_TK, relu2=False,
           scale=None, res=None, rope=None, rope_width=None):
    M = a.shape[0]
    _, K, N = w3.shape
    tm, tn, tk = min(tm, M), min(tn, N), min(tk, K)
    in_specs = [pl.BlockSpec((tm, tk), lambda j, i: (i, kslab)),
                pl.BlockSpec((None, tk, tn), lambda j, i: (layer, kslab, j))]
    args = [a, w3]
    if res is not None:
        in_specs.append(pl.BlockSpec((tm, tn), lambda j, i: (i, j)))
        args.append(res)
    rope_tiles = 0
    if rope is not None:
        assert rope_width % tn == 0
        rope_tiles = rope_width // tn
        in_specs += [pl.BlockSpec((tm, LANES), lambda j, i: (i, 0))] * 3
        args += list(rope)
    kern = functools.partial(_mm_kernel, relu2=relu2, scale=scale,
                             has_res=res is not None, rope_tiles=rope_tiles)
    return pl.pallas_call(
        kern,
        grid=(N // tn, M // tm),
        in_specs=in_specs,
        out_specs=pl.BlockSpec((tm, tn), lambda j, i: (i, j)),
        out_shape=jax.ShapeDtypeStruct((M, N), out_dtype),
        scratch_shapes=[pltpu.VMEM((tk, tn), BF16)],
        compiler_params=_params("parallel", "arbitrary"),
        name="matmul",
    )(*args)


def _dattn_kernel(lam_ref, q_ref, k_ref, v_ref, g_ref, o_ref,
                  s_ref, p_ref, mb_ref, lp_ref, acc_ref, *, kc, lam_init):
    tq = q_ref.shape[0]
    nkc = k_ref.shape[0] // kc
    d = DA_HEAD_DIM
    lv = lam_ref[...]
    lam = (jnp.exp(jnp.sum(lv[0:1] * lv[1:2], axis=1, keepdims=True))
           - jnp.exp(jnp.sum(lv[2:3] * lv[3:4], axis=1, keepdims=True)) + lam_init)

    maps = (0, 1)
    mp = [None, None]
    for c in range(nkc):
        for m in maps:
            kk = k_ref[c * kc:(c + 1) * kc, m * d:(m + 1) * d]
            s = lax.dot_general(q_ref[:, m * d:(m + 1) * d], kk, (((1,), (1,)), ((), ())),
                                preferred_element_type=F32)
            s_ref[m, c] = s
            cm = s[:, 0:LANES]
            for t in range(1, kc // LANES):
                cm = jnp.maximum(cm, s[:, t * LANES:(t + 1) * LANES])
            mp[m] = cm if mp[m] is None else jnp.maximum(mp[m], cm)
    mb = [jnp.broadcast_to(jnp.max(mp[m], axis=1, keepdims=True), (tq, LANES)) for m in maps]
    for m in maps:
        mb_ref[m] = mb[m]
        lp_ref[m] = jnp.zeros((tq, LANES), F32)
        acc_ref[m] = jnp.zeros((tq, 2 * d), F32)

    def exp_chunk(c, slot):
        for m in maps:
            cl = None
            for t in range(kc // LANES):
                pt = jnp.exp(s_ref[m, c, :, t * LANES:(t + 1) * LANES] - mb_ref[m])
                cl = pt if cl is None else cl + pt
                p_ref[slot, m, :, t * LANES:(t + 1) * LANES] = pt.astype(BF16)
            lp_ref[m] += cl

    def pv_chunk(c, slot):
        for m in maps:
            vv = v_ref[pl.ds(pl.multiple_of(c * kc, kc), kc), :]
            acc_ref[m] += jnp.dot(p_ref[slot, m], vv, preferred_element_type=F32)

    exp_chunk(0, 0)

    def pair(i, carry):
        c = 2 * i
        pv_chunk(c, 0)
        exp_chunk(c + 1, 1)
        pv_chunk(c + 1, 1)
        exp_chunk(c + 2, 0)
        return carry

    lax.fori_loop(0, nkc // 2 - 1, pair, 0)
    pv_chunk(nkc - 2, 0)
    exp_chunk(nkc - 1, 1)
    pv_chunk(nkc - 1, 1)
    heads = [acc_ref[m] / jnp.sum(lp_ref[m], axis=1, keepdims=True) for m in maps]

    o = heads[0] - lam * heads[1]
    ms = jnp.mean(o * o, axis=1, keepdims=True)
    o = o * lax.rsqrt(ms + EPS) * g_ref[...] * (1.0 - lam_init)
    o_ref[...] = o.astype(o_ref.dtype)


def diff_attention(qkv, lam4, subln_g, layer, *, B, S, D, lam_init, tq=512, kc=512):
    T = B * S
    hw = 2 * DA_HEAD_DIM
    H = D // hw
    nq = S // tq
    assert (S // kc) % 2 == 0, "pass 2 walks the key chunks in pairs"
    g3 = subln_g.reshape(subln_g.shape[0], 1, hw)
    kern = functools.partial(_dattn_kernel, kc=kc, lam_init=lam_init)
    return pl.pallas_call(
        kern,
        grid=(B, H, nq),
        in_specs=[pl.BlockSpec((None, 4, DA_HEAD_DIM), lambda b, h, i: (layer, 0, 0)),
                  pl.BlockSpec((tq, hw), lambda b, h, i: (b * nq + i, h)),
                  pl.BlockSpec((S, hw), lambda b, h, i: (b, H + h)),
                  pl.BlockSpec((S, hw), lambda b, h, i: (b, 2 * H + h)),
                  pl.BlockSpec((None, 1, hw), lambda b, h, i: (layer, 0, 0))],
        out_specs=pl.BlockSpec((tq, hw), lambda b, h, i: (b * nq + i, h)),
        out_shape=jax.ShapeDtypeStruct((T, D), BF16),
        scratch_shapes=[pltpu.VMEM((2, S // kc, tq, kc), F32),
                        pltpu.VMEM((2, 2, tq, kc), BF16),
                        pltpu.VMEM((2, tq, LANES), F32),
                        pltpu.VMEM((2, tq, LANES), F32),
                        pltpu.VMEM((2, tq, hw), F32)],
        compiler_params=_params("parallel", "parallel", "arbitrary"),
        name="diff_attention",
    )(lam4, qkv, qkv, qkv, g3)


def _xattn_kernel(q_ref, k_ref, v_ref, o_ref):
    hd = X_HEAD_DIM
    for h in range(X_HEADS):
        q = q_ref[:, h * hd:(h + 1) * hd]
        k = k_ref[:, h * hd:(h + 1) * hd]
        v = v_ref[:, h * hd:(h + 1) * hd]
        s = lax.dot_general(q, k, (((1,), (1,)), ((), ())), preferred_element_type=F32)
        p = jnp.exp(s - jnp.max(s, axis=1, keepdims=True))
        p = p / jnp.sum(p, axis=1, keepdims=True)
        o = jnp.dot(p.astype(BF16), v, preferred_element_type=F32)
        o_ref[:, h * hd:(h + 1) * hd] = o.astype(o_ref.dtype)


def xattn_core(q, kv, *, B, S, tm=512):
    T, XW = q.shape
    M = kv.shape[0] // B
    nt = S // tm
    return pl.pallas_call(
        _xattn_kernel,
        grid=(B, nt),
        in_specs=[pl.BlockSpec((tm, XW), lambda b, i: (b * nt + i, 0)),
                  pl.BlockSpec((M, XW), lambda b, i: (b, 0)),
                  pl.BlockSpec((M, XW), lambda b, i: (b, 1))],
        out_specs=pl.BlockSpec((tm, XW), lambda b, i: (b * nt + i, 0)),
        out_shape=jax.ShapeDtypeStruct((T, XW), BF16),
        compiler_params=_params("parallel", "parallel"),
        name="xattn_core",
    )(q, kv, kv)


def _softplus(x):
    return jnp.maximum(x, 0.0) + jnp.log1p(jnp.exp(-jnp.abs(x)))


def _sigmoid(x):
    return 1.0 / (1.0 + jnp.exp(-x))


def _lru_gate_kernel(u_ref, up_ref, un_ref, cw_ref, cb_ref,
                     waf_ref, baf_ref, wif_ref, bif_ref, lamf_ref,
                     wab_ref, bab_ref, wib_ref, bib_ref, lamb_ref,
                     af_ref, xf_ref, ab_ref, xb_ref, pad_ref):
    i = pl.program_id(2)
    ni = pl.num_programs(2)
    tc = u_ref.shape[0]
    h = SUBLANES
    u = u_ref[...]
    pad_ref[0:h] = jnp.where(i > 0, up_ref[...], 0.0)
    pad_ref[h:h + tc] = u
    pad_ref[h + tc:2 * h + tc] = jnp.where(i < ni - 1, un_ref[...], 0.0)
    cw = cw_ref[...]
    uc = cb_ref[...]
    for t in range(CONV_WIDTH):
        off = h + t - CONV_LEFT
        uc = uc + pad_ref[off:off + tc] * cw[t:t + 1]
    ub = uc.astype(BF16)

    def gate(w_ref, b_ref):
        z = jnp.dot(ub, w_ref[...].astype(BF16), preferred_element_type=F32) + b_ref[...]
        return _sigmoid(z)

    for wa, ba, wi, bi, lam, a_out, x_out in (
            (waf_ref, baf_ref, wif_ref, bif_ref, lamf_ref, af_ref, xf_ref),
            (wab_ref, bab_ref, wib_ref, bib_ref, lamb_ref, ab_ref, xb_ref)):
        r = gate(wa, ba)
        ig = gate(wi, bi)
        log_a = (-LRU_C) * r * _softplus(-lam[...])
        a = jnp.exp(log_a)
        a_out[...] = a
        x_out[...] = jnp.sqrt(jnp.tanh(-log_a) * (1.0 + a * a)) * (ig * uc)


def lru_gates(ug, layer, conv_w, conv_b, dirs, *, B, S, W, tc=1024):
    T = B * S
    bw = LRU_BLOCK_W
    nb = W // bw
    tc = min(tc, S)
    nt = S // tc
    h = SUBLANES
    last8 = T // h - 1

    def u_map(n, b, i):
        return (b * nt + i, n)

    def prev_map(n, b, i):
        return (jnp.maximum((b * S + i * tc) // h - 1, 0), n)

    def next_map(n, b, i):
        return (jnp.minimum((b * S + (i + 1) * tc) // h, last8), n)

    in_specs = [pl.BlockSpec((tc, bw), u_map),
                pl.BlockSpec((h, bw), prev_map),
                pl.BlockSpec((h, bw), next_map),
                pl.BlockSpec((None, CONV_WIDTH, bw), lambda n, b, i: (layer, 0, n)),
                pl.BlockSpec((None, 1, bw), lambda n, b, i: (layer, 0, n))]
    args = [ug, ug, ug, conv_w, conv_b.reshape(conv_b.shape[0], 1, W)]
    wspec = pl.BlockSpec((None, None, bw, bw), lambda n, b, i: (layer, n, 0, 0))
    vspec = pl.BlockSpec((None, None, 1, bw), lambda n, b, i: (layer, n, 0, 0))
    for wa, ba, wi, bi, lam in dirs:
        L = wa.shape[0]
        in_specs += [wspec, vspec, wspec, vspec, vspec]
        args += [wa, ba.reshape(L, nb, 1, bw), wi, bi.reshape(L, nb, 1, bw),
                 lam.reshape(L, nb, 1, bw)]
    out = jax.ShapeDtypeStruct((T, W), F32)
    ospec = pl.BlockSpec((tc, bw), u_map)
    return pl.pallas_call(
        _lru_gate_kernel,
        grid=(nb, B, nt),
        in_specs=in_specs,
        out_specs=[ospec] * 4,
        out_shape=[out] * 4,
        scratch_shapes=[pltpu.VMEM((tc + 2 * h, bw), F32)],
        compiler_params=_params("parallel", "parallel", "parallel"),
        name="lru_gates",
    )(*args)


def _lru_scan_kernel(af_ref, xf_ref, ab_ref, xb_ref, hf_ref, hb_ref, st_ref, *, unroll):
    c = pl.program_id(2)
    tc = af_ref.shape[0]

    @pl.when(c == 0)
    def _():
        st_ref[...] = jnp.zeros_like(st_ref)

    def step(t, carry):
        hf, hb = carry
        hf = af_ref[t] * hf + xf_ref[t]
        hf_ref[t] = hf
        tb = tc - 1 - t
        hb = ab_ref[tb] * hb + xb_ref[tb]
        hb_ref[tb] = hb
        return hf, hb

    hf, hb = lax.fori_loop(0, tc, step, (st_ref[0], st_ref[1]), unroll=unroll)
    st_ref[0] = hf
    st_ref[1] = hb


def lru_scan(a_f, x_f, a_b, x_b, *, B, S, W, tc=256, sl=16, unroll=8):
    wr = W // LANES
    tc, sl = min(tc, S), min(sl, wr)
    nc = S // tc
    shape4 = (B, S, wr, LANES)
    fwd = pl.BlockSpec((None, tc, sl, LANES), lambda b, s, c: (b, c, s, 0))
    bwd = pl.BlockSpec((None, tc, sl, LANES), lambda b, s, c: (b, nc - 1 - c, s, 0))
    out = jax.ShapeDtypeStruct(shape4, F32)
    hf, hb = pl.pallas_call(
        functools.partial(_lru_scan_kernel, unroll=unroll),
        grid=(B, wr // sl, nc),
        in_specs=[fwd, fwd, bwd, bwd],
        out_specs=[fwd, bwd],
        out_shape=[out, out],
        scratch_shapes=[pltpu.VMEM((2, sl, LANES), F32)],
        compiler_params=_params("parallel", "parallel", "arbitrary"),
        name="lru_scan",
    )(a_f.reshape(shape4), x_f.reshape(shape4), a_b.reshape(shape4), x_b.reshape(shape4))
    return hf.reshape(B * S, W), hb.reshape(B * S, W)


def _lru_combine_kernel(hf_ref, hb_ref, g_ref, o_ref):
    g = g_ref[...]
    gelu = 0.5 * g * (1.0 + jnp.tanh(math.sqrt(2.0 / math.pi) * (g + 0.044715 * (g * g * g))))
    o_ref[...] = ((hf_ref[...] + hb_ref[...]) * gelu).astype(o_ref.dtype)


def lru_combine(hf, hb, ug, *, tm=512, tn=1024):
    T, W = hf.shape
    tm, tn = min(tm, T), min(tn, W)
    spec = pl.BlockSpec((tm, tn), lambda i, j: (i, j))
    return pl.pallas_call(
        _lru_combine_kernel,
        grid=(T // tm, W // tn),
        in_specs=[spec, spec, pl.BlockSpec((tm, tn), lambda i, j: (i, W // tn + j))],
        out_specs=spec,
        out_shape=jax.ShapeDtypeStruct((T, W), BF16),
        compiler_params=_params("parallel", "parallel"),
        name="lru_combine",
    )(hf, hb, ug)


def _lambda_init(layer_idx):
    return 0.8 - 0.6 * math.exp(-0.3 * layer_idx)


def kernel(x, mem, positions, attn_norm_g, attn_w_qkv, attn_lambda_q1, attn_lambda_k1, attn_lambda_q2, attn_lambda_k2, attn_subln_g, attn_w_o, rnn_norm_g, rnn_w_in, rnn_conv_w, rnn_conv_b, rnn_wa_f, rnn_ba_f, rnn_wi_f, rnn_bi_f, rnn_lam_f, rnn_wa_b, rnn_ba_b, rnn_wi_b, rnn_bi_b, rnn_lam_b, rnn_w_out, xattn_norm_g, xattn_mem_g, xattn_w_q, xattn_w_kv, xattn_w_o, mlp_norm_g, mlp_w1, mlp_w2, final_g):
    B, S, D = x.shape
    T = B * S
    depth = xattn_norm_g.shape[0]
    M = mem.shape[1]
    W = rnn_lam_f.shape[-1]

    h = x.reshape(T, D)
    mem2 = mem.reshape(B * M, D)
    rope = rope_tables(positions)
    lam4 = jnp.stack([attn_lambda_q1, attn_lambda_k1, attn_lambda_q2, attn_lambda_k2], axis=1)

    for i in range(depth):
        j = i // N_MIXERS
        if i % N_MIXERS == 0:
            xn = rmsnorm(h, attn_norm_g, j, BF16)
            qkv = matmul(xn, attn_w_qkv, j, out_dtype=BF16,
                         scale=DA_HEAD_DIM ** -0.5, rope=rope, rope_width=D)
            o = diff_attention(qkv, lam4, attn_subln_g, j, B=B, S=S, D=D,
                               lam_init=_lambda_init(i))
            h = matmul(o, attn_w_o, j, out_dtype=F32, res=h)
        else:
            xn = rmsnorm(h, rnn_norm_g, j, BF16)
            ug = matmul(xn, rnn_w_in, j, out_dtype=F32)
            a_f, x_f, a_b, x_b = lru_gates(
                ug, j, rnn_conv_w, rnn_conv_b,
                ((rnn_wa_f, rnn_ba_f, rnn_wi_f, rnn_bi_f, rnn_lam_f),
                 (rnn_wa_b, rnn_ba_b, rnn_wi_b, rnn_bi_b, rnn_lam_b)),
                B=B, S=S, W=W)
            hf, hb = lru_scan(a_f, x_f, a_b, x_b, B=B, S=S, W=W)
            y = lru_combine(hf, hb, ug)
            h = matmul(y, rnn_w_out, j, out_dtype=F32, res=h)

        xn = rmsnorm(h, xattn_norm_g, i, BF16)
        q = matmul(xn, xattn_w_q, i, out_dtype=BF16, scale=X_HEAD_DIM ** -0.5)
        kv = matmul(rmsnorm(mem2, xattn_mem_g, i, BF16), xattn_w_kv, i, out_dtype=BF16)
        o = xattn_core(q, kv, B=B, S=S)
        h = matmul(o, xattn_w_o, i, out_dtype=F32, res=h, tm=512, tn=2048)

        xn = rmsnorm(h, mlp_norm_g, i, BF16)
        hid = matmul(xn, mlp_w1, i, out_dtype=BF16, relu2=True)
        for ks in range(mlp_w2.shape[1] // min(MM_TK, mlp_w2.shape[1])):
            h = matmul(hid, mlp_w2, i, out_dtype=F32, res=h, kslab=ks)

    return rmsnorm(h, final_g.reshape(1, D), 0, F32).reshape(B, S, D)
```

```python
import functools
import math

import jax
import jax.numpy as jnp
from jax import lax
from jax.experimental import pallas as pl
from jax.experimental.pallas import tpu as pltpu

F32 = jnp.float32
BF16 = jnp.bfloat16

EPS = 1e-6
ROPE_THETA = 500000.0
DA_HEAD_DIM = 128
ROT_DIM = DA_HEAD_DIM // 4
LRU_BLOCK_W = 256
LRU_C = 8.0
CONV_WIDTH = 4
CONV_LEFT = 2
X_HEADS = 4
X_HEAD_DIM = 128
N_MIXERS = 2

LANES = 128
SUBLANES = 8
VMEM_LIMIT_BYTES = 56 * 1024 * 1024
MM_TK = 4096


def _params(*semantics):
    return pltpu.CompilerParams(dimension_semantics=semantics,
                                vmem_limit_bytes=VMEM_LIMIT_BYTES)


def _rope_table_kernel(pos_ref, invf_ref, cos_ref, sa_ref, sb_ref):
    ang = pos_ref[...].astype(F32) * invf_ref[...]
    lane = lax.broadcasted_iota(jnp.int32, ang.shape, 1)
    half = ROT_DIM // 2
    c, s = jnp.cos(ang), jnp.sin(ang)
    cos_ref[...] = jnp.where(lane < ROT_DIM, c, 1.0)
    sa_ref[...] = jnp.where(lane < half, -s, 0.0)
    sb_ref[...] = jnp.where((lane >= half) & (lane < ROT_DIM), s, 0.0)


def rope_tables(positions, tm=1024):
    T = positions.size
    tm = min(tm, T)
    half = ROT_DIM // 2
    inv_freq = ROPE_THETA ** (-jnp.arange(0, ROT_DIM, 2, dtype=F32) / ROT_DIM)
    invf = jnp.tile(inv_freq, LANES // half)[None, :]
    tab = jax.ShapeDtypeStruct((T, LANES), F32)
    spec = pl.BlockSpec((tm, LANES), lambda i: (i, 0))
    return pl.pallas_call(
        _rope_table_kernel,
        grid=(T // tm,),
        in_specs=[pl.BlockSpec((tm, 1), lambda i: (i, 0)),
                  pl.BlockSpec((1, LANES), lambda i: (0, 0))],
        out_specs=[spec, spec, spec],
        out_shape=[tab, tab, tab],
        compiler_params=_params("parallel"),
        name="rope_tables",
    )(positions.reshape(T, 1), invf)


def _rmsnorm_kernel(x_ref, g_ref, o_ref):
    x = x_ref[...].astype(F32)
    ms = jnp.mean(x * x, axis=-1, keepdims=True)
    o_ref[...] = (x * lax.rsqrt(ms + EPS) * g_ref[...]).astype(o_ref.dtype)


def rmsnorm(x, g2, layer, out_dtype, tm=256):
    T, D = x.shape
    tm = min(tm, T)
    g3 = g2.reshape(g2.shape[0], 1, D)
    return pl.pallas_call(
        _rmsnorm_kernel,
        grid=(T // tm,),
        in_specs=[pl.BlockSpec((tm, D), lambda i: (i, 0)),
                  pl.BlockSpec((None, 1, D), lambda i: (layer, 0, 0))],
        out_specs=pl.BlockSpec((tm, D), lambda i: (i, 0)),
        out_shape=jax.ShapeDtypeStruct((T, D), out_dtype),
        compiler_params=_params("parallel"),
        name="rmsnorm",
    )(x, g3)


def _rope_cols(t, cos, sa, sb):
    half = ROT_DIM // 2
    return (t * cos + pltpu.roll(t, LANES - half, 1) * sa + pltpu.roll(t, half, 1) * sb)


def _mm_kernel(*refs, relu2, scale, has_res, rope_tiles):
    it = iter(refs)
    a_ref, w_ref = next(it), next(it)
    res_ref = next(it) if has_res else None
    if rope_tiles:
        cos_ref, sa_ref, sb_ref = next(it), next(it), next(it)
    o_ref, wbf_ref = next(it), next(it)

    @pl.when(pl.program_id(1) == 0)
    def _():
        wbf_ref[...] = w_ref[...].astype(BF16)

    acc = jnp.dot(a_ref[...], wbf_ref[...], preferred_element_type=F32)

    if rope_tiles:
        j = pl.program_id(0)
        rotary = j < 2 * rope_tiles
        qs = jnp.where(j < rope_tiles, scale, 1.0).astype(F32)
        cos = jnp.where(rotary, cos_ref[...], 1.0) * qs
        sa = jnp.where(rotary, sa_ref[...], 0.0) * qs
        sb = jnp.where(rotary, sb_ref[...], 0.0) * qs
        for c in range(acc.shape[1] // LANES):
            t = _rope_cols(acc[:, c * LANES:(c + 1) * LANES], cos, sa, sb)
            o_ref[:, c * LANES:(c + 1) * LANES] = t.astype(o_ref.dtype)
        return
    if relu2:
        r = jnp.maximum(acc, 0.0)
        acc = r * r
    if scale is not None:
        acc = acc * scale
    if has_res:
        acc = acc + res_ref[...]
    o_ref[...] = acc.astype(o_ref.dtype)


def matmul(a, w3, layer, *, out_dtype, kslab=0, tm=1024, tn=512, tk=MM_TK, relu2=False,
           scale=None, res=None, rope=None, rope_width=None):
    M = a.shape[0]
    _, K, N = w3.shape
    tm, tn, tk = min(tm, M), min(tn, N), min(tk, K)
    in_specs = [pl.BlockSpec((tm, tk), lambda j, i: (i, kslab)),
                pl.BlockSpec((None, tk, tn), lambda j, i: (layer, kslab, j))]
    args = [a, w3]
    if res is not None:
        in_specs.append(pl.BlockSpec((tm, tn), lambda j, i: (i, j)))
        args.append(res)
    rope_tiles = 0
    if rope is not None:
        assert rope_width % tn == 0
        rope_tiles = rope_width // tn
        in_specs += [pl.BlockSpec((tm, LANES), lambda j, i: (i, 0))] * 3
        args += list(rope)
    kern = functools.partial(_mm_kernel, relu2=relu2, scale=scale,
                             has_res=res is not None, rope_tiles=rope_tiles)
    return pl.pallas_call(
        kern,
        grid=(N // tn, M // tm),
        in_specs=in_specs,
        out_specs=pl.BlockSpec((tm, tn), lambda j, i: (i, j)),
        out_shape=jax.ShapeDtypeStruct((M, N), out_dtype),
        scratch_shapes=[pltpu.VMEM((tk, tn), BF16)],
        compiler_params=_params("parallel", "arbitrary"),
        name="matmul",
    )(*args)


def _dattn_kernel(lam_ref, q_ref, k_ref, v_ref, g_ref, o_ref,
                  s_ref, p_ref, mb_ref, lp_ref, acc_ref, *, kc, lam_init):
    tq = q_ref.shape[0]
    nkc = k_ref.shape[0] // kc
    d = DA_HEAD_DIM
    lv = lam_ref[...]
    lam = (jnp.exp(jnp.sum(lv[0:1] * lv[1:2], axis=1, keepdims=True))
           - jnp.exp(jnp.sum(lv[2:3] * lv[3:4], axis=1, keepdims=True)) + lam_init)

    maps = (0, 1)
    mp = [None, None]
    for c in range(nkc):
        for m in maps:
            kk = k_ref[c * kc:(c + 1) * kc, m * d:(m + 1) * d]
            s = lax.dot_general(q_ref[:, m * d:(m + 1) * d], kk, (((1,), (1,)), ((), ())),
                                preferred_element_type=F32)
            s_ref[m, c] = s
            cm = s[:, 0:LANES]
            for t in range(1, kc // LANES):
                cm = jnp.maximum(cm, s[:, t * LANES:(t + 1) * LANES])
            mp[m] = cm if mp[m] is None else jnp.maximum(mp[m], cm)
    mb = [jnp.broadcast_to(jnp.max(mp[m], axis=1, keepdims=True), (tq, LANES)) for m in maps]
    for m in maps:
        mb_ref[m] = mb[m]
        lp_ref[m] = jnp.zeros((tq, LANES), F32)
        acc_ref[m] = jnp.zeros((tq, 2 * d), F32)

    def exp_chunk(c, slot):
        for m in maps:
            cl = None
            for t in range(kc // LANES):
                pt = jnp.exp(s_ref[m, c, :, t * LANES:(t + 1) * LANES] - mb_ref[m])
                cl = pt if cl is None else cl + pt
                p_ref[slot, m, :, t * LANES:(t + 1) * LANES] = pt.astype(BF16)
            lp_ref[m] += cl

    def pv_chunk(c, slot):
        for m in maps:
            vv = v_ref[pl.ds(pl.multiple_of(c * kc, kc), kc), :]
            acc_ref[m] += jnp.dot(p_ref[slot, m], vv, preferred_element_type=F32)

    exp_chunk(0, 0)

    def pair(i, carry):
        c = 2 * i
        pv_chunk(c, 0)
        exp_chunk(c + 1, 1)
        pv_chunk(c + 1, 1)
        exp_chunk(c + 2, 0)
        return carry

    lax.fori_loop(0, nkc // 2 - 1, pair, 0)
    pv_chunk(nkc - 2, 0)
    exp_chunk(nkc - 1, 1)
    pv_chunk(nkc - 1, 1)
    heads = [acc_ref[m] / jnp.sum(lp_ref[m], axis=1, keepdims=True) for m in maps]

    o = heads[0] - lam * heads[1]
    ms = jnp.mean(o * o, axis=1, keepdims=True)
    o = o * lax.rsqrt(ms + EPS) * g_ref[...] * (1.0 - lam_init)
    o_ref[...] = o.astype(o_ref.dtype)


def diff_attention(qkv, lam4, subln_g, layer, *, B, S, D, lam_init, tq=512, kc=512):
    T = B * S
    hw = 2 * DA_HEAD_DIM
    H = D // hw
    nq = S // tq
    assert (S // kc) % 2 == 0, "pass 2 walks the key chunks in pairs"
    g3 = subln_g.reshape(subln_g.shape[0], 1, hw)
    kern = functools.partial(_dattn_kernel, kc=kc, lam_init=lam_init)
    return pl.pallas_call(
        kern,
        grid=(B, H, nq),
        in_specs=[pl.BlockSpec((None, 4, DA_HEAD_DIM), lambda b, h, i: (layer, 0, 0)),
                  pl.BlockSpec((tq, hw), lambda b, h, i: (b * nq + i, h)),
                  pl.BlockSpec((S, hw), lambda b, h, i: (b, H + h)),
                  pl.BlockSpec((S, hw), lambda b, h, i: (b, 2 * H + h)),
                  pl.BlockSpec((None, 1, hw), lambda b, h, i: (layer, 0, 0))],
        out_specs=pl.BlockSpec((tq, hw), lambda b, h, i: (b * nq + i, h)),
        out_shape=jax.ShapeDtypeStruct((T, D), BF16),
        scratch_shapes=[pltpu.VMEM((2, S // kc, tq, kc), F32),
                        pltpu.VMEM((2, 2, tq, kc), BF16),
                        pltpu.VMEM((2, tq, LANES), F32),
                        pltpu.VMEM((2, tq, LANES), F32),
                        pltpu.VMEM((2, tq, hw), F32)],
        compiler_params=_params("parallel", "parallel", "arbitrary"),
        name="diff_attention",
    )(lam4, qkv, qkv, qkv, g3)


def _xattn_kernel(q_ref, k_ref, v_ref, o_ref):
    hd = X_HEAD_DIM
    for h in range(X_HEADS):
        q = q_ref[:, h * hd:(h + 1) * hd]
        k = k_ref[:, h * hd:(h + 1) * hd]
        v = v_ref[:, h * hd:(h + 1) * hd]
        s = lax.dot_general(q, k, (((1,), (1,)), ((), ())), preferred_element_type=F32)
        p = jnp.exp(s - jnp.max(s, axis=1, keepdims=True))
        p = p / jnp.sum(p, axis=1, keepdims=True)
        o = jnp.dot(p.astype(BF16), v, preferred_element_type=F32)
        o_ref[:, h * hd:(h + 1) * hd] = o.astype(o_ref.dtype)


def xattn_core(q, kv, *, B, S, tm=512):
    T, XW = q.shape
    M = kv.shape[0] // B
    nt = S // tm
    return pl.pallas_call(
        _xattn_kernel,
        grid=(B, nt),
        in_specs=[pl.BlockSpec((tm, XW), lambda b, i: (b * nt + i, 0)),
                  pl.BlockSpec((M, XW), lambda b, i: (b, 0)),
                  pl.BlockSpec((M, XW), lambda b, i: (b, 1))],
        out_specs=pl.BlockSpec((tm, XW), lambda b, i: (b * nt + i, 0)),
        out_shape=jax.ShapeDtypeStruct((T, XW), BF16),
        compiler_params=_params("parallel", "parallel"),
        name="xattn_core",
    )(q, kv, kv)


def _softplus(x):
    return jnp.maximum(x, 0.0) + jnp.log1p(jnp.exp(-jnp.abs(x)))


def _sigmoid(x):
    return 1.0 / (1.0 + jnp.exp(-x))


def _gelu_tanh(g):
    return 0.5 * g * (1.0 + jnp.tanh(math.sqrt(2.0 / math.pi) * (g + 0.044715 * (g * g * g))))


def _blocked_scan(a_ref, x_ref, h_ref, gp_ref, gh_ref, ent_ref, l3_ref, d3_ref, e3_ref,
                  h0, reverse):
    tc = a_ref.shape[0]
    g = tc // SUBLANES
    n3 = g // SUBLANES
    order = tuple(range(SUBLANES - 1, -1, -1)) if reverse else tuple(range(SUBLANES))
    korder = tuple(range(n3 - 1, -1, -1)) if reverse else tuple(range(n3))
    last = order[-1]

    def strided(ref, r, n):
        return ref[pl.ds(r, n, stride=SUBLANES), :]

    loc, dec = {}, {}
    prev = None
    for r in order:
        a, x = strided(a_ref, r, g), strided(x_ref, r, g)
        loc[r] = x if prev is None else a * loc[prev] + x
        dec[r] = a if prev is None else a * dec[prev]
        prev = r
    gp_ref[...] = dec[last]
    gh_ref[...] = loc[last]

    loc2, dec2 = {}, {}
    prev = None
    for s in order:
        q, k = strided(gp_ref, s, n3), strided(gh_ref, s, n3)
        loc2[s] = k if prev is None else q * loc2[prev] + k
        dec2[s] = q if prev is None else q * dec2[prev]
        prev = s

    state = h0
    l3_ref[...] = loc2[last]
    d3_ref[...] = dec2[last]
    for k in korder:
        e3_ref[k:k + 1, :] = state
        state = l3_ref[k:k + 1, :] + d3_ref[k:k + 1, :] * state
    entering3 = e3_ref[...]

    entering = entering3
    for s in order:
        ent_ref[pl.ds(s, n3, stride=SUBLANES), :] = entering
        entering = loc2[s] + dec2[s] * entering3
    ent = ent_ref[...]
    for r in order:
        h_ref[pl.ds(r, g, stride=SUBLANES), :] = loc[r] + dec[r] * ent
    return state


def _lru_kernel(u_ref, up_ref, un_ref, gate_ref, cw_ref, cb_ref,
                waf_ref, baf_ref, wif_ref, bif_ref, lamf_ref,
                wab_ref, bab_ref, wib_ref, bib_ref, lamb_ref,
                y_ref, pad_ref, a_ref, x_ref, h_ref, gp_ref, gh_ref, ent_ref,
                l3_ref, d3_ref, e3_ref, hf_ref, uc_ref, st_ref):
    ph = pl.program_id(2)
    c = pl.program_id(3)
    nt = pl.num_programs(3)
    cc = jnp.where(ph == 0, c, nt - 1 - c)
    tc, bw = u_ref.shape
    h = SUBLANES
    row0 = pl.multiple_of(cc * tc, tc)
    lane_tiles = [slice(l * LANES, (l + 1) * LANES) for l in range(bw // LANES)]

    @pl.when(c == 0)
    def _():
        st_ref[...] = jnp.zeros_like(st_ref)

    def conv():
        pad_ref[0:h] = jnp.where(cc > 0, up_ref[...], 0.0)
        pad_ref[h:h + tc] = u_ref[...]
        pad_ref[h + tc:2 * h + tc] = jnp.where(cc < nt - 1, un_ref[...], 0.0)
        cw = cw_ref[...]
        uc = cb_ref[...]
        for t in range(CONV_WIDTH):
            off = h + t - CONV_LEFT
            uc = uc + pad_ref[off:off + tc] * cw[t:t + 1]
        return uc

    def sweep(uc, wa, ba, wi, bi, lam, reverse):
        ub = uc.astype(BF16)

        def gate(w_ref, b_ref):
            z = jnp.dot(ub, w_ref[...].astype(BF16), preferred_element_type=F32) + b_ref[...]
            return _sigmoid(z)

        r = gate(wa, ba)
        ig = gate(wi, bi)
        log_a = (-LRU_C) * r * _softplus(-lam[...])
        a = jnp.exp(log_a)
        x = jnp.sqrt(jnp.tanh(-log_a) * (1.0 + a * a)) * (ig * uc)
        for l, lanes in enumerate(lane_tiles):
            a_ref[l] = a[:, lanes]
            x_ref[l] = x[:, lanes]
        for l, lanes in enumerate(lane_tiles):
            st_ref[0:1, lanes] = _blocked_scan(
                a_ref.at[l], x_ref.at[l], h_ref.at[l], gp_ref.at[l], gh_ref.at[l], ent_ref.at[l],
                l3_ref.at[l], d3_ref.at[l], e3_ref.at[l], st_ref[0:1, lanes], reverse)

    @pl.when(ph == 0)
    def _():
        uc = conv()
        uc_ref[pl.ds(row0, tc), :] = uc
        sweep(uc, waf_ref, baf_ref, wif_ref, bif_ref, lamf_ref, False)
        for l in range(len(lane_tiles)):
            hf_ref[l, pl.ds(row0, tc), :] = h_ref[l]

    @pl.when(ph == 1)
    def _():
        sweep(uc_ref[pl.ds(row0, tc), :], wab_ref, bab_ref, wib_ref, bib_ref, lamb_ref, True)
        for l, lanes in enumerate(lane_tiles):
            hsum = hf_ref[l, pl.ds(row0, tc), :] + h_ref[l]
            y_ref[:, lanes] = (hsum * _gelu_tanh(gate_ref[:, lanes])).astype(y_ref.dtype)


def rglru(ug, layer, conv_w, conv_b, dirs, *, B, S, W, tc=1024):
    T = B * S
    bw = LRU_BLOCK_W
    nb = W // bw
    tc = min(tc, S)
    nt = S // tc
    h = SUBLANES
    assert tc % (h * h) == 0 and S % tc == 0
    last8 = T // h - 1

    def early(p, c):
        return c + p * (nt - 1 - c)

    def late(p, c):
        return nt - 1 - p * c

    in_specs = [
        pl.BlockSpec((tc, bw), lambda n, b, p, c: (b * nt + early(p, c), n)),
        pl.BlockSpec((h, bw), lambda n, b, p, c: (
            jnp.maximum((b * S + early(p, c) * tc) // h - 1, 0), n)),
        pl.BlockSpec((h, bw), lambda n, b, p, c: (
            jnp.minimum((b * S + (early(p, c) + 1) * tc) // h, last8), n)),
        pl.BlockSpec((tc, bw), lambda n, b, p, c: (b * nt + late(p, c), nb + n)),
        pl.BlockSpec((None, CONV_WIDTH, bw), lambda n, b, p, c: (layer, 0, n)),
        pl.BlockSpec((None, 1, bw), lambda n, b, p, c: (layer, 0, n))]
    args = [ug, ug, ug, ug, conv_w, conv_b.reshape(conv_b.shape[0], 1, W)]
    wspec = pl.BlockSpec((None, None, bw, bw), lambda n, b, p, c: (layer, n, 0, 0))
    vspec = pl.BlockSpec((None, None, 1, bw), lambda n, b, p, c: (layer, n, 0, 0))
    for wa, ba, wi, bi, lam in dirs:
        L = wa.shape[0]
        in_specs += [wspec, vspec, wspec, vspec, vspec]
        args += [wa, ba.reshape(L, nb, 1, bw), wi, bi.reshape(L, nb, 1, bw),
                 lam.reshape(L, nb, 1, bw)]
    g = tc // h
    nl = bw // LANES

    def tiles(rows):
        return pltpu.VMEM((nl, rows, LANES), F32)

    return pl.pallas_call(
        _lru_kernel,
        grid=(nb, B, 2, nt),
        in_specs=in_specs,
        out_specs=pl.BlockSpec((tc, bw), lambda n, b, p, c: (b * nt + late(p, c), n)),
        out_shape=jax.ShapeDtypeStruct((T, W), BF16),
        scratch_shapes=[pltpu.VMEM((tc + 2 * h, bw), F32),
                        tiles(tc),
                        tiles(tc),
                        tiles(tc),
                        tiles(g),
                        tiles(g),
                        tiles(g),
                        tiles(g // h),
                        tiles(g // h),
                        tiles(g // h),
                        tiles(S),
                        pltpu.VMEM((S, bw), F32),
                        pltpu.VMEM((h, bw), F32)],
        compiler_params=_params("parallel", "parallel", "arbitrary", "arbitrary"),
        name="rglru",
    )(*args)


def _lambda_init(layer_idx):
    return 0.8 - 0.6 * math.exp(-0.3 * layer_idx)


def kernel(x, mem, positions, attn_norm_g, attn_w_qkv, attn_lambda_q1, attn_lambda_k1, attn_lambda_q2, attn_lambda_k2, attn_subln_g, attn_w_o, rnn_norm_g, rnn_w_in, rnn_conv_w, rnn_conv_b, rnn_wa_f, rnn_ba_f, rnn_wi_f, rnn_bi_f, rnn_lam_f, rnn_wa_b, rnn_ba_b, rnn_wi_b, rnn_bi_b, rnn_lam_b, rnn_w_out, xattn_norm_g, xattn_mem_g, xattn_w_q, xattn_w_kv, xattn_w_o, mlp_norm_g, mlp_w1, mlp_w2, final_g):
    B, S, D = x.shape
    T = B * S
    depth = xattn_norm_g.shape[0]
    M = mem.shape[1]
    W = rnn_lam_f.shape[-1]

    h = x.reshape(T, D)
    mem2 = mem.reshape(B * M, D)
    rope = rope_tables(positions)
    lam4 = jnp.stack([attn_lambda_q1, attn_lambda_k1, attn_lambda_q2, attn_lambda_k2], axis=1)

    for i in range(depth):
        j = i // N_MIXERS
        if i % N_MIXERS == 0:
            xn = rmsnorm(h, attn_norm_g, j, BF16)
            qkv = matmul(xn, attn_w_qkv, j, out_dtype=BF16,
                         scale=DA_HEAD_DIM ** -0.5, rope=rope, rope_width=D)
            o = diff_attention(qkv, lam4, attn_subln_g, j, B=B, S=S, D=D,
                               lam_init=_lambda_init(i))
            h = matmul(o, attn_w_o, j, out_dtype=F32, res=h)
        else:
            xn = rmsnorm(h, rnn_norm_g, j, BF16)
            ug = matmul(xn, rnn_w_in, j, out_dtype=F32)
            y = rglru(ug, j, rnn_conv_w, rnn_conv_b,
                      ((rnn_wa_f, rnn_ba_f, rnn_wi_f, rnn_bi_f, rnn_lam_f),
                       (rnn_wa_b, rnn_ba_b, rnn_wi_b, rnn_bi_b, rnn_lam_b)),
                      B=B, S=S, W=W)
            h = matmul(y, rnn_w_out, j, out_dtype=F32, res=h)

        xn = rmsnorm(h, xattn_norm_g, i, BF16)
        q = matmul(xn, xattn_w_q, i, out_dtype=BF16, scale=X_HEAD_DIM ** -0.5)
        kv = matmul(rmsnorm(mem2, xattn_mem_g, i, BF16), xattn_w_kv, i, out_dtype=BF16)
        o = xattn_core(q, kv, B=B, S=S)
        h = matmul(o, xattn_w_o, i, out_dtype=F32, res=h, tm=512, tn=2048)

        xn = rmsnorm(h, mlp_norm_g, i, BF16)
        hid = matmul(xn, mlp_w1, i, out_dtype=BF16, relu2=True)
        for ks in range(mlp_w2.shape[1] // min(MM_TK, mlp_w2.shape[1])):
            h = matmul(hid, mlp_w2, i, out_dtype=F32, res=h, kslab=ks)

    return rmsnorm(h, final_g.reshape(1, D), 0, F32).reshape(B, S, D)
```

```python
import functools
import math

import jax
import jax.numpy as jnp
from jax import lax
from jax.experimental import pallas as pl
from jax.experimental.pallas import tpu as pltpu

F32 = jnp.float32
BF16 = jnp.bfloat16

EPS = 1e-6
ROPE_THETA = 500000.0
DA_HEAD_DIM = 128
ROT_DIM = DA_HEAD_DIM // 4
LRU_BLOCK_W = 256
LRU_C = 8.0
CONV_WIDTH = 4
CONV_LEFT = 2
X_HEADS = 4
X_HEAD_DIM = 128
N_MIXERS = 2

LANES = 128
SUBLANES = 8
VMEM_LIMIT_BYTES = 56 * 1024 * 1024
MM_TK = 4096
EXP_PANEL = 64


def _params(*semantics):
    return pltpu.CompilerParams(dimension_semantics=semantics,
                                vmem_limit_bytes=VMEM_LIMIT_BYTES)


def _rope_table_kernel(pos_ref, invf_ref, cos_ref, sa_ref, sb_ref):
    ang = pos_ref[...].astype(F32) * invf_ref[...]
    lane = lax.broadcasted_iota(jnp.int32, ang.shape, 1)
    half = ROT_DIM // 2
    c, s = jnp.cos(ang), jnp.sin(ang)
    cos_ref[...] = jnp.where(lane < ROT_DIM, c, 1.0)
    sa_ref[...] = jnp.where(lane < half, -s, 0.0)
    sb_ref[...] = jnp.where((lane >= half) & (lane < ROT_DIM), s, 0.0)


def rope_tables(positions, tm=1024):
    T = positions.size
    tm = min(tm, T)
    half = ROT_DIM // 2
    inv_freq = ROPE_THETA ** (-jnp.arange(0, ROT_DIM, 2, dtype=F32) / ROT_DIM)
    invf = jnp.tile(inv_freq, LANES // half)[None, :]
    tab = jax.ShapeDtypeStruct((T, LANES), F32)
    spec = pl.BlockSpec((tm, LANES), lambda i: (i, 0))
    return pl.pallas_call(
        _rope_table_kernel,
        grid=(T // tm,),
        in_specs=[pl.BlockSpec((tm, 1), lambda i: (i, 0)),
                  pl.BlockSpec((1, LANES), lambda i: (0, 0))],
        out_specs=[spec, spec, spec],
        out_shape=[tab, tab, tab],
        compiler_params=_params("parallel"),
        name="rope_tables",
    )(positions.reshape(T, 1), invf)


def _rmsnorm_kernel(x_ref, g_ref, o_ref):
    x = x_ref[...].astype(F32)
    ms = jnp.mean(x * x, axis=-1, keepdims=True)
    o_ref[...] = (x * lax.rsqrt(ms + EPS) * g_ref[...]).astype(o_ref.dtype)


def rmsnorm(x, g2, layer, out_dtype, tm=256):
    T, D = x.shape
    tm = min(tm, T)
    g3 = g2.reshape(g2.shape[0], 1, D)
    return pl.pallas_call(
        _rmsnorm_kernel,
        grid=(T // tm,),
        in_specs=[pl.BlockSpec((tm, D), lambda i: (i, 0)),
                  pl.BlockSpec((None, 1, D), lambda i: (layer, 0, 0))],
        out_specs=pl.BlockSpec((tm, D), lambda i: (i, 0)),
        out_shape=jax.ShapeDtypeStruct((T, D), out_dtype),
        compiler_params=_params("parallel"),
        name="rmsnorm",
    )(x, g3)


def _rope_cols(t, cos, sa, sb):
    half = ROT_DIM // 2
    return (t * cos + pltpu.roll(t, LANES - half, 1) * sa + pltpu.roll(t, half, 1) * sb)


def _mm_kernel(*refs, relu2, scale, has_res, rope_tiles):
    it = iter(refs)
    a_ref, w_ref = next(it), next(it)
    res_ref = next(it) if has_res else None
    if rope_tiles:
        cos_ref, sa_ref, sb_ref = next(it), next(it), next(it)
    o_ref, wbf_ref = next(it), next(it)

    @pl.when(pl.program_id(1) == 0)
    def _():
        wbf_ref[...] = w_ref[...].astype(BF16)

    acc = jnp.dot(a_ref[...], wbf_ref[...], preferred_element_type=F32)

    if rope_tiles:
        j = pl.program_id(0)
        rotary = j < 2 * rope_tiles
        qs = jnp.where(j < rope_tiles, scale, 1.0).astype(F32)
        cos = jnp.where(rotary, cos_ref[...], 1.0) * qs
        sa = jnp.where(rotary, sa_ref[...], 0.0) * qs
        sb = jnp.where(rotary, sb_ref[...], 0.0) * qs
        for c in range(acc.shape[1] // LANES):
            t = _rope_cols(acc[:, c * LANES:(c + 1) * LANES], cos, sa, sb)
            o_ref[:, c * LANES:(c + 1) * LANES] = t.astype(o_ref.dtype)
        return
    if relu2:
        r = jnp.maximum(acc, 0.0)
        acc = r * r
    if scale is not None:
        acc = acc * scale
    if has_res:
        acc = acc + res_ref[...]
    o_ref[...] = acc.astype(o_ref.dtype)


def matmul(a, w3, layer, *, out_dtype, kslab=0, tm=1024, tn=512, tk=MM_TK, relu2=False,
           scale=None, res=None, rope=None, rope_width=None):
    M = a.shape[0]
    _, K, N = w3.shape
    tm, tn, tk = min(tm, M), min(tn, N), min(tk, K)
    in_specs = [pl.BlockSpec((tm, tk), lambda j, i: (i, kslab)),
                pl.BlockSpec((None, tk, tn), lambda j, i: (layer, kslab, j))]
    args = [a, w3]
    if res is not None:
        in_specs.append(pl.BlockSpec((tm, tn), lambda j, i: (i, j)))
        args.append(res)
    rope_tiles = 0
    if rope is not None:
        assert rope_width % tn == 0
        rope_tiles = rope_width // tn
        in_specs += [pl.BlockSpec((tm, LANES), lambda j, i: (i, 0))] * 3
        args += list(rope)
    kern = functools.partial(_mm_kernel, relu2=relu2, scale=scale,
                             has_res=res is not None, rope_tiles=rope_tiles)
    return pl.pallas_call(
        kern,
        grid=(N // tn, M // tm),
        in_specs=in_specs,
        out_specs=pl.BlockSpec((tm, tn), lambda j, i: (i, j)),
        out_shape=jax.ShapeDtypeStruct((M, N), out_dtype),
        scratch_shapes=[pltpu.VMEM((tk, tn), BF16)],
        compiler_params=_params("parallel", "arbitrary"),
        name="matmul",
    )(*args)


def _dattn_kernel(lam_ref, q_ref, k_ref, v_ref, g_ref, o_ref, s_ref, p_ref, *, kc, rb, lam_init):
    tq = q_ref.shape[0]
    nkc = k_ref.shape[0] // kc
    nrb = tq // rb
    npan = rb // EXP_PANEL
    d = DA_HEAD_DIM
    lv = lam_ref[...]
    lam = (jnp.exp(jnp.sum(lv[0:1] * lv[1:2], axis=1, keepdims=True))
           - jnp.exp(jnp.sum(lv[2:3] * lv[3:4], axis=1, keepdims=True)) + lam_init)
    maps = (0, 1)

    def scores(r, c, mp):
        for m in maps:
            kk = k_ref[c * kc:(c + 1) * kc, m * d:(m + 1) * d]
            s = lax.dot_general(q_ref[r * rb:(r + 1) * rb, m * d:(m + 1) * d], kk,
                                (((1,), (1,)), ((), ())), preferred_element_type=F32)
            s_ref[m, c] = s
            cm = s[:, 0:LANES]
            for t in range(1, kc // LANES):
                cm = jnp.maximum(cm, s[:, t * LANES:(t + 1) * LANES])
            mp[m] = cm if mp[m] is None else jnp.maximum(mp[m], cm)

    def exps(c, mb, lp):
        for m in maps:
            for i in range(rb // EXP_PANEL):
                rows = slice(i * EXP_PANEL, (i + 1) * EXP_PANEL)
                cl = None
                for t in range(kc // LANES):
                    pt = jnp.exp(s_ref[m, c, rows, t * LANES:(t + 1) * LANES] - mb[m][i])
                    cl = pt if cl is None else cl + pt
                    p_ref[m, rows, c * kc + t * LANES:c * kc + (t + 1) * LANES] = pt.astype(BF16)
                lp[m][i] = cl if lp[m][i] is None else lp[m][i] + cl

    def weighted(r, lp):
        hk = k_ref.shape[0] // 2
        heads = []
        for m in maps:
            pv = (jnp.dot(p_ref[m, :, 0:hk], v_ref[0:hk, :], preferred_element_type=F32)
                  + jnp.dot(p_ref[m, :, hk:], v_ref[hk:, :], preferred_element_type=F32))
            l = jnp.concatenate([jnp.sum(x, axis=1, keepdims=True) for x in lp[m]], axis=0)
            heads.append(pv / l)
        o = heads[0] - lam * heads[1]
        ms = jnp.mean(o * o, axis=1, keepdims=True)
        o = o * lax.rsqrt(ms + EPS) * g_ref[...] * (1.0 - lam_init)
        o_ref[r * rb:(r + 1) * rb, :] = o.astype(o_ref.dtype)

    mp, mb, lp = {}, {}, {}
    for stage in range(nrb + 2):
        ra, re, rp = stage, stage - 1, stage - 2
        if 0 <= rp < nrb:
            weighted(rp, lp[rp])
        if 0 <= ra < nrb:
            mp[ra] = [None, None]
        if 0 <= re < nrb:
            lp[re] = [[None] * npan for _ in maps]
        for c in range(nkc):
            if 0 <= re < nrb:
                exps(c, mb[re], lp[re])
            if 0 <= ra < nrb:
                scores(ra, c, mp[ra])
        if 0 <= ra < nrb:
            mb[ra] = []
            for m in maps:
                rowmax = jnp.max(mp[ra][m], axis=1, keepdims=True)
                mb[ra].append([jnp.broadcast_to(rowmax[i * EXP_PANEL:(i + 1) * EXP_PANEL],
                                                (EXP_PANEL, LANES)) for i in range(npan)])


def diff_attention(qkv, lam4, subln_g, layer, *, B, S, D, lam_init, tq=1024, rb=512, kc=512):
    T = B * S
    hw = 2 * DA_HEAD_DIM
    H = D // hw
    tq = min(tq, S)
    rb = min(rb, tq)
    nq = S // tq
    g3 = subln_g.reshape(subln_g.shape[0], 1, hw)
    kern = functools.partial(_dattn_kernel, kc=kc, rb=rb, lam_init=lam_init)
    return pl.pallas_call(
        kern,
        grid=(B, H, nq),
        in_specs=[pl.BlockSpec((None, 4, DA_HEAD_DIM), lambda b, h, i: (layer, 0, 0)),
                  pl.BlockSpec((tq, hw), lambda b, h, i: (b * nq + i, h)),
                  pl.BlockSpec((S, hw), lambda b, h, i: (b, H + h)),
                  pl.BlockSpec((S, hw), lambda b, h, i: (b, 2 * H + h)),
                  pl.BlockSpec((None, 1, hw), lambda b, h, i: (layer, 0, 0))],
        out_specs=pl.BlockSpec((tq, hw), lambda b, h, i: (b * nq + i, h)),
        out_shape=jax.ShapeDtypeStruct((T, D), BF16),
        scratch_shapes=[pltpu.VMEM((2, S // kc, rb, kc), F32),
                        pltpu.VMEM((2, rb, S), BF16)],
        compiler_params=_params("parallel", "parallel", "arbitrary"),
        name="diff_attention",
    )(lam4, qkv, qkv, qkv, g3)


def _xattn_kernel(q_ref, k_ref, v_ref, o_ref):
    hd = X_HEAD_DIM
    for h in range(X_HEADS):
        q = q_ref[:, h * hd:(h + 1) * hd]
        k = k_ref[:, h * hd:(h + 1) * hd]
        v = v_ref[:, h * hd:(h + 1) * hd]
        s = lax.dot_general(q, k, (((1,), (1,)), ((), ())), preferred_element_type=F32)
        p = jnp.exp(s - jnp.max(s, axis=1, keepdims=True))
        p = p / jnp.sum(p, axis=1, keepdims=True)
        o = jnp.dot(p.astype(BF16), v, preferred_element_type=F32)
        o_ref[:, h * hd:(h + 1) * hd] = o.astype(o_ref.dtype)


def xattn_core(q, kv, *, B, S, tm=512):
    T, XW = q.shape
    M = kv.shape[0] // B
    nt = S // tm
    return pl.pallas_call(
        _xattn_kernel,
        grid=(B, nt),
        in_specs=[pl.BlockSpec((tm, XW), lambda b, i: (b * nt + i, 0)),
                  pl.BlockSpec((M, XW), lambda b, i: (b, 0)),
                  pl.BlockSpec((M, XW), lambda b, i: (b, 1))],
        out_specs=pl.BlockSpec((tm, XW), lambda b, i: (b * nt + i, 0)),
        out_shape=jax.ShapeDtypeStruct((T, XW), BF16),
        compiler_params=_params("parallel", "parallel"),
        name="xattn_core",
    )(q, kv, kv)


def _softplus(x):
    return jnp.maximum(x, 0.0) + jnp.log1p(jnp.exp(-jnp.abs(x)))


def _sigmoid(x):
    return 1.0 / (1.0 + jnp.exp(-x))


def _gelu_tanh(g):
    return 0.5 * g * (1.0 + jnp.tanh(math.sqrt(2.0 / math.pi) * (g + 0.044715 * (g * g * g))))


def _blocked_scan(a_ref, x_ref, h_ref, gp_ref, gh_ref, ent_ref, l3_ref, d3_ref, e3_ref,
                  h0, reverse):
    tc = a_ref.shape[0]
    g = tc // SUBLANES
    n3 = g // SUBLANES
    order = tuple(range(SUBLANES - 1, -1, -1)) if reverse else tuple(range(SUBLANES))
    korder = tuple(range(n3 - 1, -1, -1)) if reverse else tuple(range(n3))
    last = order[-1]

    def strided(ref, r, n):
        return ref[pl.ds(r, n, stride=SUBLANES), :]

    loc, dec = {}, {}
    prev = None
    for r in order:
        a, x = strided(a_ref, r, g), strided(x_ref, r, g)
        loc[r] = x if prev is None else a * loc[prev] + x
        dec[r] = a if prev is None else a * dec[prev]
        prev = r
    gp_ref[...] = dec[last]
    gh_ref[...] = loc[last]

    loc2, dec2 = {}, {}
    prev = None
    for s in order:
        q, k = strided(gp_ref, s, n3), strided(gh_ref, s, n3)
        loc2[s] = k if prev is None else q * loc2[prev] + k
        dec2[s] = q if prev is None else q * dec2[prev]
        prev = s

    state = h0
    l3_ref[...] = loc2[last]
    d3_ref[...] = dec2[last]
    for k in korder:
        e3_ref[k:k + 1, :] = state
        state = l3_ref[k:k + 1, :] + d3_ref[k:k + 1, :] * state
    entering3 = e3_ref[...]

    entering = entering3
    for s in order:
        ent_ref[pl.ds(s, n3, stride=SUBLANES), :] = entering
        entering = loc2[s] + dec2[s] * entering3
    ent = ent_ref[...]
    for r in order:
        h_ref[pl.ds(r, g, stride=SUBLANES), :] = loc[r] + dec[r] * ent
    return state


def _lru_kernel(u_ref, up_ref, un_ref, gate_ref, cw_ref, cb_ref,
                waf_ref, baf_ref, wif_ref, bif_ref, lamf_ref,
                wab_ref, bab_ref, wib_ref, bib_ref, lamb_ref,
                y_ref, pad_ref, a_ref, x_ref, h_ref, gp_ref, gh_ref, ent_ref,
                l3_ref, d3_ref, e3_ref, hf_ref, uc_ref, st_ref):
    ph = pl.program_id(2)
    c = pl.program_id(3)
    nt = pl.num_programs(3)
    cc = jnp.where(ph == 0, c, nt - 1 - c)
    tc, bw = u_ref.shape
    h = SUBLANES
    row0 = pl.multiple_of(cc * tc, tc)
    lane_tiles = [slice(l * LANES, (l + 1) * LANES) for l in range(bw // LANES)]

    @pl.when(c == 0)
    def _():
        st_ref[...] = jnp.zeros_like(st_ref)

    def conv():
        pad_ref[0:h] = jnp.where(cc > 0, up_ref[...], 0.0)
        pad_ref[h:h + tc] = u_ref[...]
        pad_ref[h + tc:2 * h + tc] = jnp.where(cc < nt - 1, un_ref[...], 0.0)
        cw = cw_ref[...]
        uc = cb_ref[...]
        for t in range(CONV_WIDTH):
            off = h + t - CONV_LEFT
            uc = uc + pad_ref[off:off + tc] * cw[t:t + 1]
        return uc

    def sweep(uc, wa, ba, wi, bi, lam, reverse):
        ub = uc.astype(BF16)

        def gate(w_ref, b_ref):
            z = jnp.dot(ub, w_ref[...].astype(BF16), preferred_element_type=F32) + b_ref[...]
            return _sigmoid(z)

        r = gate(wa, ba)
        ig = gate(wi, bi)
        log_a = (-LRU_C) * r * _softplus(-lam[...])
        a = jnp.exp(log_a)
        x = jnp.sqrt(jnp.tanh(-log_a) * (1.0 + a * a)) * (ig * uc)
        for l, lanes in enumerate(lane_tiles):
            a_ref[l] = a[:, lanes]
            x_ref[l] = x[:, lanes]
        for l, lanes in enumerate(lane_tiles):
            st_ref[0:1, lanes] = _blocked_scan(
                a_ref.at[l], x_ref.at[l], h_ref.at[l], gp_ref.at[l], gh_ref.at[l], ent_ref.at[l],
                l3_ref.at[l], d3_ref.at[l], e3_ref.at[l], st_ref[0:1, lanes], reverse)

    @pl.when(ph == 0)
    def _():
        uc = conv()
        uc_ref[pl.ds(row0, tc), :] = uc
        sweep(uc, waf_ref, baf_ref, wif_ref, bif_ref, lamf_ref, False)
        for l in range(len(lane_tiles)):
            hf_ref[l, pl.ds(row0, tc), :] = h_ref[l]

    @pl.when(ph == 1)
    def _():
        sweep(uc_ref[pl.ds(row0, tc), :], wab_ref, bab_ref, wib_ref, bib_ref, lamb_ref, True)
        for l, lanes in enumerate(lane_tiles):
            hsum = hf_ref[l, pl.ds(row0, tc), :] + h_ref[l]
            y_ref[:, lanes] = (hsum * _gelu_tanh(gate_ref[:, lanes])).astype(y_ref.dtype)


def rglru(ug, layer, conv_w, conv_b, dirs, *, B, S, W, tc=1024):
    T = B * S
    bw = LRU_BLOCK_W
    nb = W // bw
    tc = min(tc, S)
    nt = S // tc
    h = SUBLANES
    assert tc % (h * h) == 0 and S % tc == 0
    last8 = T // h - 1

    def early(p, c):
        return c + p * (nt - 1 - c)

    def late(p, c):
        return nt - 1 - p * c

    in_specs = [
        pl.BlockSpec((tc, bw), lambda n, b, p, c: (b * nt + early(p, c), n)),
        pl.BlockSpec((h, bw), lambda n, b, p, c: (
            jnp.maximum((b * S + early(p, c) * tc) // h - 1, 0), n)),
        pl.BlockSpec((h, bw), lambda n, b, p, c: (
            jnp.minimum((b * S + (early(p, c) + 1) * tc) // h, last8), n)),
        pl.BlockSpec((tc, bw), lambda n, b, p, c: (b * nt + late(p, c), nb + n)),
        pl.BlockSpec((None, CONV_WIDTH, bw), lambda n, b, p, c: (layer, 0, n)),
        pl.BlockSpec((None, 1, bw), lambda n, b, p, c: (layer, 0, n))]
    args = [ug, ug, ug, ug, conv_w, conv_b.reshape(conv_b.shape[0], 1, W)]
    wspec = pl.BlockSpec((None, None, bw, bw), lambda n, b, p, c: (layer, n, 0, 0))
    vspec = pl.BlockSpec((None, None, 1, bw), lambda n, b, p, c: (layer, n, 0, 0))
    for wa, ba, wi, bi, lam in dirs:
        L = wa.shape[0]
        in_specs += [wspec, vspec, wspec, vspec, vspec]
        args += [wa, ba.reshape(L, nb, 1, bw), wi, bi.reshape(L, nb, 1, bw),
                 lam.reshape(L, nb, 1, bw)]
    g = tc // h
    nl = bw // LANES

    def tiles(rows):
        return pltpu.VMEM((nl, rows, LANES), F32)

    return pl.pallas_call(
        _lru_kernel,
        grid=(nb, B, 2, nt),
        in_specs=in_specs,
        out_specs=pl.BlockSpec((tc, bw), lambda n, b, p, c: (b * nt + late(p, c), n)),
        out_shape=jax.ShapeDtypeStruct((T, W), BF16),
        scratch_shapes=[pltpu.VMEM((tc + 2 * h, bw), F32),
                        tiles(tc),
                        tiles(tc),
                        tiles(tc),
                        tiles(g),
                        tiles(g),
                        tiles(g),
                        tiles(g // h),
                        tiles(g // h),
                        tiles(g // h),
                        tiles(S),
                        pltpu.VMEM((S, bw), F32),
                        pltpu.VMEM((h, bw), F32)],
        compiler_params=_params("parallel", "parallel", "arbitrary", "arbitrary"),
        name="rglru",
    )(*args)


def _lambda_init(layer_idx):
    return 0.8 - 0.6 * math.exp(-0.3 * layer_idx)


def kernel(x, mem, positions, attn_norm_g, attn_w_qkv, attn_lambda_q1, attn_lambda_k1, attn_lambda_q2, attn_lambda_k2, attn_subln_g, attn_w_o, rnn_norm_g, rnn_w_in, rnn_conv_w, rnn_conv_b, rnn_wa_f, rnn_ba_f, rnn_wi_f, rnn_bi_f, rnn_lam_f, rnn_wa_b, rnn_ba_b, rnn_wi_b, rnn_bi_b, rnn_lam_b, rnn_w_out, xattn_norm_g, xattn_mem_g, xattn_w_q, xattn_w_kv, xattn_w_o, mlp_norm_g, mlp_w1, mlp_w2, final_g):
    B, S, D = x.shape
    T = B * S
    depth = xattn_norm_g.shape[0]
    M = mem.shape[1]
    W = rnn_lam_f.shape[-1]

    h = x.reshape(T, D)
    mem2 = mem.reshape(B * M, D)
    rope = rope_tables(positions)
    lam4 = jnp.stack([attn_lambda_q1, attn_lambda_k1, attn_lambda_q2, attn_lambda_k2], axis=1)

    for i in range(depth):
        j = i // N_MIXERS
        if i % N_MIXERS == 0:
            xn = rmsnorm(h, attn_norm_g, j, BF16)
            qkv = matmul(xn, attn_w_qkv, j, out_dtype=BF16,
                         scale=DA_HEAD_DIM ** -0.5, rope=rope, rope_width=D)
            o = diff_attention(qkv, lam4, attn_subln_g, j, B=B, S=S, D=D,
                               lam_init=_lambda_init(i))
            h = matmul(o, attn_w_o, j, out_dtype=F32, res=h)
        else:
            xn = rmsnorm(h, rnn_norm_g, j, BF16)
            ug = matmul(xn, rnn_w_in, j, out_dtype=F32)
            y = rglru(ug, j, rnn_conv_w, rnn_conv_b,
                      ((rnn_wa_f, rnn_ba_f, rnn_wi_f, rnn_bi_f, rnn_lam_f),
                       (rnn_wa_b, rnn_ba_b, rnn_wi_b, rnn_bi_b, rnn_lam_b)),
                      B=B, S=S, W=W)
            h = matmul(y, rnn_w_out, j, out_dtype=F32, res=h)

        xn = rmsnorm(h, xattn_norm_g, i, BF16)
        q = matmul(xn, xattn_w_q, i, out_dtype=BF16, scale=X_HEAD_DIM ** -0.5)
        kv = matmul(rmsnorm(mem2, xattn_mem_g, i, BF16), xattn_w_kv, i, out_dtype=BF16)
        o = xattn_core(q, kv, B=B, S=S)
        h = matmul(o, xattn_w_o, i, out_dtype=F32, res=h, tm=512, tn=2048)

        xn = rmsnorm(h, mlp_norm_g, i, BF16)
        hid = matmul(xn, mlp_w1, i, out_dtype=BF16, relu2=True)
        for ks in range(mlp_w2.shape[1] // min(MM_TK, mlp_w2.shape[1])):
            h = matmul(hid, mlp_w2, i, out_dtype=F32, res=h, kslab=ks)

    return rmsnorm(h, final_g.reshape(1, D), 0, F32).reshape(B, S, D)
```

```python
import functools
import math

import jax
import jax.numpy as jnp
from jax import lax
from jax.experimental import pallas as pl
from jax.experimental.pallas import tpu as pltpu

F32 = jnp.float32
BF16 = jnp.bfloat16

EPS = 1e-6
ROPE_THETA = 500000.0
DA_HEAD_DIM = 128
ROT_DIM = DA_HEAD_DIM // 4
LRU_BLOCK_W = 256
LRU_C = 8.0
CONV_WIDTH = 4
CONV_LEFT = 2
X_HEADS = 4
X_HEAD_DIM = 128
N_MIXERS = 2

LANES = 128
SUBLANES = 8
VMEM_LIMIT_BYTES = 56 * 1024 * 1024
MM_TK = 4096
EXP_PANEL = 64


def _params(*semantics):
    return pltpu.CompilerParams(dimension_semantics=semantics,
                                vmem_limit_bytes=VMEM_LIMIT_BYTES)


def _rope_table_kernel(pos_ref, invf_ref, cos_ref, sa_ref, sb_ref):
    ang = pos_ref[...].astype(F32) * invf_ref[...]
    lane = lax.broadcasted_iota(jnp.int32, ang.shape, 1)
    half = ROT_DIM // 2
    c, s = jnp.cos(ang), jnp.sin(ang)
    cos_ref[...] = jnp.where(lane < ROT_DIM, c, 1.0)
    sa_ref[...] = jnp.where(lane < half, -s, 0.0)
    sb_ref[...] = jnp.where((lane >= half) & (lane < ROT_DIM), s, 0.0)


def rope_tables(positions, tm=1024):
    T = positions.size
    tm = min(tm, T)
    half = ROT_DIM // 2
    inv_freq = ROPE_THETA ** (-jnp.arange(0, ROT_DIM, 2, dtype=F32) / ROT_DIM)
    invf = jnp.tile(inv_freq, LANES // half)[None, :]
    tab = jax.ShapeDtypeStruct((T, LANES), F32)
    spec = pl.BlockSpec((tm, LANES), lambda i: (i, 0))
    return pl.pallas_call(
        _rope_table_kernel,
        grid=(T // tm,),
        in_specs=[pl.BlockSpec((tm, 1), lambda i: (i, 0)),
                  pl.BlockSpec((1, LANES), lambda i: (0, 0))],
        out_specs=[spec, spec, spec],
        out_shape=[tab, tab, tab],
        compiler_params=_params("parallel"),
        name="rope_tables",
    )(positions.reshape(T, 1), invf)


def _rmsnorm_kernel(x_ref, g_ref, o_ref):
    x = x_ref[...].astype(F32)
    ms = jnp.mean(x * x, axis=-1, keepdims=True)
    o_ref[...] = (x * lax.rsqrt(ms + EPS) * g_ref[...]).astype(o_ref.dtype)


def rmsnorm(x, g2, layer, out_dtype, tm=256):
    T, D = x.shape
    tm = min(tm, T)
    g3 = g2.reshape(g2.shape[0], 1, D)
    return pl.pallas_call(
        _rmsnorm_kernel,
        grid=(T // tm,),
        in_specs=[pl.BlockSpec((tm, D), lambda i: (i, 0)),
                  pl.BlockSpec((None, 1, D), lambda i: (layer, 0, 0))],
        out_specs=pl.BlockSpec((tm, D), lambda i: (i, 0)),
        out_shape=jax.ShapeDtypeStruct((T, D), out_dtype),
        compiler_params=_params("parallel"),
        name="rmsnorm",
    )(x, g3)


def _rope_cols(t, cos, sa, sb):
    half = ROT_DIM // 2
    return (t * cos + pltpu.roll(t, LANES - half, 1) * sa + pltpu.roll(t, half, 1) * sb)


def _mm_kernel(*refs, relu2, scale, has_res, rope_tiles, norm_dim, stats_out):
    it = iter(refs)
    a_ref, w_ref = next(it), next(it)
    res_ref = next(it) if has_res else None
    if rope_tiles:
        cos_ref, sa_ref, sb_ref = next(it), next(it), next(it)
    if norm_dim:
        g_ref, ssq_ref = next(it), next(it)
    o_ref = next(it)
    if stats_out:
        ob_ref, osq_ref = next(it), next(it)
    wbf_ref = next(it)
    if stats_out:
        acc_sq_ref = next(it)
    j, i = pl.program_id(0), pl.program_id(1)

    @pl.when(i == 0)
    def _():
        w = w_ref[...]
        if norm_dim:
            w = w * g_ref[...]
        wbf_ref[...] = w.astype(BF16)

    acc = jnp.dot(a_ref[...], wbf_ref[...], preferred_element_type=F32)
    if norm_dim:
        ssq = jnp.sum(ssq_ref[...], axis=1, keepdims=True)
        acc = acc * lax.rsqrt(ssq * (1.0 / norm_dim) + EPS)

    if rope_tiles:
        rotary = j < 2 * rope_tiles
        qs = jnp.where(j < rope_tiles, scale, 1.0).astype(F32)
        cos = jnp.where(rotary, cos_ref[...], 1.0) * qs
        sa = jnp.where(rotary, sa_ref[...], 0.0) * qs
        sb = jnp.where(rotary, sb_ref[...], 0.0) * qs
        for c in range(acc.shape[1] // LANES):
            t = _rope_cols(acc[:, c * LANES:(c + 1) * LANES], cos, sa, sb)
            o_ref[:, c * LANES:(c + 1) * LANES] = t.astype(o_ref.dtype)
        return
    if relu2:
        r = jnp.maximum(acc, 0.0)
        acc = r * r
    if scale is not None:
        acc = acc * scale
    if has_res:
        acc = acc + res_ref[...]
    o_ref[...] = acc.astype(o_ref.dtype)
    if stats_out:
        ob_ref[...] = acc.astype(BF16)
        sq = acc * acc
        part = sq[:, 0:LANES]
        for c in range(1, acc.shape[1] // LANES):
            part = part + sq[:, c * LANES:(c + 1) * LANES]
        total = jnp.where(j == 0, 0.0, acc_sq_ref[i]) + part
        acc_sq_ref[i] = total
        osq_ref[...] = total


def matmul(a, w3, layer, *, out_dtype, kslab=0, tm=1024, tn=512, tk=MM_TK, relu2=False,
           scale=None, res=None, rope=None, rope_width=None, norm=None, stats_out=False):
    M = a.shape[0]
    _, K, N = w3.shape
    tm, tn, tk = min(tm, M), min(tn, N), min(tk, K)
    nj, ni = N // tn, M // tm
    in_specs = [pl.BlockSpec((tm, tk), lambda j, i: (i, kslab)),
                pl.BlockSpec((None, tk, tn), lambda j, i: (layer, kslab, j))]
    args = [a, w3]
    if res is not None:
        in_specs.append(pl.BlockSpec((tm, tn), lambda j, i: (i, j)))
        args.append(res)
    rope_tiles = 0
    if rope is not None:
        assert rope_width % tn == 0
        rope_tiles = rope_width // tn
        in_specs += [pl.BlockSpec((tm, LANES), lambda j, i: (i, 0))] * 3
        args += list(rope)
    if norm is not None:
        assert tk == K, "the folded rmsnorm needs the whole row in one contraction slab"
        g, ssq = norm
        in_specs += [pl.BlockSpec((tk, 1), lambda j, i: (0, 0)),
                     pl.BlockSpec((tm, LANES), lambda j, i: (i, 0))]
        args += [g.reshape(K, 1), ssq]
    out_specs = pl.BlockSpec((tm, tn), lambda j, i: (i, j))
    out_shape = jax.ShapeDtypeStruct((M, N), out_dtype)
    scratch = [pltpu.VMEM((tk, tn), BF16)]
    if stats_out:
        out_specs = [out_specs, pl.BlockSpec((tm, tn), lambda j, i: (i, j)),
                     pl.BlockSpec((tm, LANES), lambda j, i: (jnp.where(j == nj - 1, i, 0), 0))]
        out_shape = [out_shape, jax.ShapeDtypeStruct((M, N), BF16),
                     jax.ShapeDtypeStruct((M, LANES), F32)]
        scratch.append(pltpu.VMEM((ni, tm, LANES), F32))
    kern = functools.partial(_mm_kernel, relu2=relu2, scale=scale, has_res=res is not None,
                             rope_tiles=rope_tiles, norm_dim=K if norm is not None else 0,
                             stats_out=stats_out)
    return pl.pallas_call(
        kern,
        grid=(nj, ni),
        in_specs=in_specs,
        out_specs=out_specs,
        out_shape=out_shape,
        scratch_shapes=scratch,
        compiler_params=_params("arbitrary", "arbitrary"),
        name="matmul",
    )(*args)


def _dattn_kernel(lam_ref, q_ref, k_ref, v_ref, g_ref, o_ref, s_ref, p_ref, *, kc, rb, lam_init):
    tq = q_ref.shape[0]
    nkc = k_ref.shape[0] // kc
    nrb = tq // rb
    npan = rb // EXP_PANEL
    d = DA_HEAD_DIM
    lv = lam_ref[...]
    lam = (jnp.exp(jnp.sum(lv[0:1] * lv[1:2], axis=1, keepdims=True))
           - jnp.exp(jnp.sum(lv[2:3] * lv[3:4], axis=1, keepdims=True)) + lam_init)
    maps = (0, 1)

    def scores(r, c, mp):
        for m in maps:
            kk = k_ref[c * kc:(c + 1) * kc, m * d:(m + 1) * d]
            s = lax.dot_general(q_ref[r * rb:(r + 1) * rb, m * d:(m + 1) * d], kk,
                                (((1,), (1,)), ((), ())), preferred_element_type=F32)
            s_ref[m, c] = s
            cm = s[:, 0:LANES]
            for t in range(1, kc // LANES):
                cm = jnp.maximum(cm, s[:, t * LANES:(t + 1) * LANES])
            mp[m] = cm if mp[m] is None else jnp.maximum(mp[m], cm)

    def exps(c, mb, lp):
        for m in maps:
            for i in range(rb // EXP_PANEL):
                rows = slice(i * EXP_PANEL, (i + 1) * EXP_PANEL)
                cl = None
                for t in range(kc // LANES):
                    pt = jnp.exp(s_ref[m, c, rows, t * LANES:(t + 1) * LANES] - mb[m][i])
                    cl = pt if cl is None else cl + pt
                    p_ref[m, rows, c * kc + t * LANES:c * kc + (t + 1) * LANES] = pt.astype(BF16)
                lp[m][i] = cl if lp[m][i] is None else lp[m][i] + cl

    def weighted(r, lp):
        hk = k_ref.shape[0] // 2
        heads = []
        for m in maps:
            pv = (jnp.dot(p_ref[m, :, 0:hk], v_ref[0:hk, :], preferred_element_type=F32)
                  + jnp.dot(p_ref[m, :, hk:], v_ref[hk:, :], preferred_element_type=F32))
            l = jnp.concatenate([jnp.sum(x, axis=1, keepdims=True) for x in lp[m]], axis=0)
            heads.append(pv / l)
        o = heads[0] - lam * heads[1]
        ms = jnp.mean(o * o, axis=1, keepdims=True)
        o = o * lax.rsqrt(ms + EPS) * g_ref[...] * (1.0 - lam_init)
        o_ref[r * rb:(r + 1) * rb, :] = o.astype(o_ref.dtype)

    mp, mb, lp = {}, {}, {}
    for stage in range(nrb + 2):
        ra, re, rp = stage, stage - 1, stage - 2
        if 0 <= rp < nrb:
            weighted(rp, lp[rp])
        if 0 <= ra < nrb:
            mp[ra] = [None, None]
        if 0 <= re < nrb:
            lp[re] = [[None] * npan for _ in maps]
        for c in range(nkc):
            if 0 <= re < nrb:
                exps(c, mb[re], lp[re])
            if 0 <= ra < nrb:
                scores(ra, c, mp[ra])
        if 0 <= ra < nrb:
            mb[ra] = []
            for m in maps:
                rowmax = jnp.max(mp[ra][m], axis=1, keepdims=True)
                mb[ra].append([jnp.broadcast_to(rowmax[i * EXP_PANEL:(i + 1) * EXP_PANEL],
                                                (EXP_PANEL, LANES)) for i in range(npan)])


def diff_attention(qkv, lam4, subln_g, layer, *, B, S, D, lam_init, tq=1024, rb=512, kc=512):
    T = B * S
    hw = 2 * DA_HEAD_DIM
    H = D // hw
    tq = min(tq, S)
    rb = min(rb, tq)
    nq = S // tq
    g3 = subln_g.reshape(subln_g.shape[0], 1, hw)
    kern = functools.partial(_dattn_kernel, kc=kc, rb=rb, lam_init=lam_init)
    return pl.pallas_call(
        kern,
        grid=(B, H, nq),
        in_specs=[pl.BlockSpec((None, 4, DA_HEAD_DIM), lambda b, h, i: (layer, 0, 0)),
                  pl.BlockSpec((tq, hw), lambda b, h, i: (b * nq + i, h)),
                  pl.BlockSpec((S, hw), lambda b, h, i: (b, H + h)),
                  pl.BlockSpec((S, hw), lambda b, h, i: (b, 2 * H + h)),
                  pl.BlockSpec((None, 1, hw), lambda b, h, i: (layer, 0, 0))],
        out_specs=pl.BlockSpec((tq, hw), lambda b, h, i: (b * nq + i, h)),
        out_shape=jax.ShapeDtypeStruct((T, D), BF16),
        scratch_shapes=[pltpu.VMEM((2, S // kc, rb, kc), F32),
                        pltpu.VMEM((2, rb, S), BF16)],
        compiler_params=_params("parallel", "parallel", "arbitrary"),
        name="diff_attention",
    )(lam4, qkv, qkv, qkv, g3)


def _xattn_kernel(q_ref, k_ref, v_ref, o_ref):
    hd = X_HEAD_DIM
    for h in range(X_HEADS):
        q = q_ref[:, h * hd:(h + 1) * hd]
        k = k_ref[:, h * hd:(h + 1) * hd]
        v = v_ref[:, h * hd:(h + 1) * hd]
        s = lax.dot_general(q, k, (((1,), (1,)), ((), ())), preferred_element_type=F32)
        p = jnp.exp(s - jnp.max(s, axis=1, keepdims=True))
        p = p / jnp.sum(p, axis=1, keepdims=True)
        o = jnp.dot(p.astype(BF16), v, preferred_element_type=F32)
        o_ref[:, h * hd:(h + 1) * hd] = o.astype(o_ref.dtype)


def xattn_core(q, kv, *, B, S, tm=512):
    T, XW = q.shape
    M = kv.shape[0] // B
    nt = S // tm
    return pl.pallas_call(
        _xattn_kernel,
        grid=(B, nt),
        in_specs=[pl.BlockSpec((tm, XW), lambda b, i: (b * nt + i, 0)),
                  pl.BlockSpec((M, XW), lambda b, i: (b, 0)),
                  pl.BlockSpec((M, XW), lambda b, i: (b, 1))],
        out_specs=pl.BlockSpec((tm, XW), lambda b, i: (b * nt + i, 0)),
        out_shape=jax.ShapeDtypeStruct((T, XW), BF16),
        compiler_params=_params("parallel", "parallel"),
        name="xattn_core",
    )(q, kv, kv)


def _softplus(x):
    return jnp.maximum(x, 0.0) + jnp.log1p(jnp.exp(-jnp.abs(x)))


def _sigmoid(x):
    return 1.0 / (1.0 + jnp.exp(-x))


def _gelu_tanh(g):
    return 0.5 * g * (1.0 + jnp.tanh(math.sqrt(2.0 / math.pi) * (g + 0.044715 * (g * g * g))))


def _blocked_scan(a_ref, x_ref, h_ref, gp_ref, gh_ref, ent_ref, l3_ref, d3_ref, e3_ref,
                  h0, reverse):
    tc = a_ref.shape[0]
    g = tc // SUBLANES
    n3 = g // SUBLANES
    order = tuple(range(SUBLANES - 1, -1, -1)) if reverse else tuple(range(SUBLANES))
    korder = tuple(range(n3 - 1, -1, -1)) if reverse else tuple(range(n3))
    last = order[-1]

    def strided(ref, r, n):
        return ref[pl.ds(r, n, stride=SUBLANES), :]

    loc, dec = {}, {}
    prev = None
    for r in order:
        a, x = strided(a_ref, r, g), strided(x_ref, r, g)
        loc[r] = x if prev is None else a * loc[prev] + x
        dec[r] = a if prev is None else a * dec[prev]
        prev = r
    gp_ref[...] = dec[last]
    gh_ref[...] = loc[last]

    loc2, dec2 = {}, {}
    prev = None
    for s in order:
        q, k = strided(gp_ref, s, n3), strided(gh_ref, s, n3)
        loc2[s] = k if prev is None else q * loc2[prev] + k
        dec2[s] = q if prev is None else q * dec2[prev]
        prev = s

    state = h0
    l3_ref[...] = loc2[last]
    d3_ref[...] = dec2[last]
    for k in korder:
        e3_ref[k:k + 1, :] = state
        state = l3_ref[k:k + 1, :] + d3_ref[k:k + 1, :] * state
    entering3 = e3_ref[...]

    entering = entering3
    for s in order:
        ent_ref[pl.ds(s, n3, stride=SUBLANES), :] = entering
        entering = loc2[s] + dec2[s] * entering3
    ent = ent_ref[...]
    for r in order:
        h_ref[pl.ds(r, g, stride=SUBLANES), :] = loc[r] + dec[r] * ent
    return state


def _lru_kernel(u_ref, up_ref, un_ref, gate_ref, cw_ref, cb_ref,
                waf_ref, baf_ref, wif_ref, bif_ref, lamf_ref,
                wab_ref, bab_ref, wib_ref, bib_ref, lamb_ref,
                y_ref, pad_ref, a_ref, x_ref, h_ref, gp_ref, gh_ref, ent_ref,
                l3_ref, d3_ref, e3_ref, hf_ref, uc_ref, st_ref):
    ph = pl.program_id(2)
    c = pl.program_id(3)
    nt = pl.num_programs(3)
    cc = jnp.where(ph == 0, c, nt - 1 - c)
    tc, bw = u_ref.shape
    h = SUBLANES
    row0 = pl.multiple_of(cc * tc, tc)
    lane_tiles = [slice(l * LANES, (l + 1) * LANES) for l in range(bw // LANES)]

    @pl.when(c == 0)
    def _():
        st_ref[...] = jnp.zeros_like(st_ref)

    def conv():
        pad_ref[0:h] = jnp.where(cc > 0, up_ref[...], 0.0)
        pad_ref[h:h + tc] = u_ref[...]
        pad_ref[h + tc:2 * h + tc] = jnp.where(cc < nt - 1, un_ref[...], 0.0)
        cw = cw_ref[...]
        uc = cb_ref[...]
        for t in range(CONV_WIDTH):
            off = h + t - CONV_LEFT
            uc = uc + pad_ref[off:off + tc] * cw[t:t + 1]
        return uc

    def sweep(uc, wa, ba, wi, bi, lam, reverse):
        ub = uc.astype(BF16)

        def gate(w_ref, b_ref):
            z = jnp.dot(ub, w_ref[...].astype(BF16), preferred_element_type=F32) + b_ref[...]
            return _sigmoid(z)

        r = gate(wa, ba)
        ig = gate(wi, bi)
        log_a = (-LRU_C) * r * _softplus(-lam[...])
        a = jnp.exp(log_a)
        x = jnp.sqrt(jnp.tanh(-log_a) * (1.0 + a * a)) * (ig * uc)
        for l, lanes in enumerate(lane_tiles):
            a_ref[l] = a[:, lanes]
            x_ref[l] = x[:, lanes]
        for l, lanes in enumerate(lane_tiles):
            st_ref[0:1, lanes] = _blocked_scan(
                a_ref.at[l], x_ref.at[l], h_ref.at[l], gp_ref.at[l], gh_ref.at[l], ent_ref.at[l],
                l3_ref.at[l], d3_ref.at[l], e3_ref.at[l], st_ref[0:1, lanes], reverse)

    @pl.when(ph == 0)
    def _():
        uc = conv()
        uc_ref[pl.ds(row0, tc), :] = uc
        sweep(uc, waf_ref, baf_ref, wif_ref, bif_ref, lamf_ref, False)
        for l in range(len(lane_tiles)):
            hf_ref[l, pl.ds(row0, tc), :] = h_ref[l]

    @pl.when(ph == 1)
    def _():
        sweep(uc_ref[pl.ds(row0, tc), :], wab_ref, bab_ref, wib_ref, bib_ref, lamb_ref, True)
        for l, lanes in enumerate(lane_tiles):
            hsum = hf_ref[l, pl.ds(row0, tc), :] + h_ref[l]
            y_ref[:, lanes] = (hsum * _gelu_tanh(gate_ref[:, lanes])).astype(y_ref.dtype)


def rglru(ug, layer, conv_w, conv_b, dirs, *, B, S, W, tc=1024):
    T = B * S
    bw = LRU_BLOCK_W
    nb = W // bw
    tc = min(tc, S)
    nt = S // tc
    h = SUBLANES
    assert tc % (h * h) == 0 and S % tc == 0
    last8 = T // h - 1

    def early(p, c):
        return c + p * (nt - 1 - c)

    def late(p, c):
        return nt - 1 - p * c

    in_specs = [
        pl.BlockSpec((tc, bw), lambda n, b, p, c: (b * nt + early(p, c), n)),
        pl.BlockSpec((h, bw), lambda n, b, p, c: (
            jnp.maximum((b * S + early(p, c) * tc) // h - 1, 0), n)),
        pl.BlockSpec((h, bw), lambda n, b, p, c: (
            jnp.minimum((b * S + (early(p, c) + 1) * tc) // h, last8), n)),
        pl.BlockSpec((tc, bw), lambda n, b, p, c: (b * nt + late(p, c), nb + n)),
        pl.BlockSpec((None, CONV_WIDTH, bw), lambda n, b, p, c: (layer, 0, n)),
        pl.BlockSpec((None, 1, bw), lambda n, b, p, c: (layer, 0, n))]
    args = [ug, ug, ug, ug, conv_w, conv_b.reshape(conv_b.shape[0], 1, W)]
    wspec = pl.BlockSpec((None, None, bw, bw), lambda n, b, p, c: (layer, n, 0, 0))
    vspec = pl.BlockSpec((None, None, 1, bw), lambda n, b, p, c: (layer, n, 0, 0))
    for wa, ba, wi, bi, lam in dirs:
        L = wa.shape[0]
        in_specs += [wspec, vspec, wspec, vspec, vspec]
        args += [wa, ba.reshape(L, nb, 1, bw), wi, bi.reshape(L, nb, 1, bw),
                 lam.reshape(L, nb, 1, bw)]
    g = tc // h
    nl = bw // LANES

    def tiles(rows):
        return pltpu.VMEM((nl, rows, LANES), F32)

    return pl.pallas_call(
        _lru_kernel,
        grid=(nb, B, 2, nt),
        in_specs=in_specs,
        out_specs=pl.BlockSpec((tc, bw), lambda n, b, p, c: (b * nt + late(p, c), n)),
        out_shape=jax.ShapeDtypeStruct((T, W), BF16),
        scratch_shapes=[pltpu.VMEM((tc + 2 * h, bw), F32),
                        tiles(tc),
                        tiles(tc),
                        tiles(tc),
                        tiles(g),
                        tiles(g),
                        tiles(g),
                        tiles(g // h),
                        tiles(g // h),
                        tiles(g // h),
                        tiles(S),
                        pltpu.VMEM((S, bw), F32),
                        pltpu.VMEM((h, bw), F32)],
        compiler_params=_params("parallel", "parallel", "arbitrary", "arbitrary"),
        name="rglru",
    )(*args)


def _lambda_init(layer_idx):
    return 0.8 - 0.6 * math.exp(-0.3 * layer_idx)


def kernel(x, mem, positions, attn_norm_g, attn_w_qkv, attn_lambda_q1, attn_lambda_k1, attn_lambda_q2, attn_lambda_k2, attn_subln_g, attn_w_o, rnn_norm_g, rnn_w_in, rnn_conv_w, rnn_conv_b, rnn_wa_f, rnn_ba_f, rnn_wi_f, rnn_bi_f, rnn_lam_f, rnn_wa_b, rnn_ba_b, rnn_wi_b, rnn_bi_b, rnn_lam_b, rnn_w_out, xattn_norm_g, xattn_mem_g, xattn_w_q, xattn_w_kv, xattn_w_o, mlp_norm_g, mlp_w1, mlp_w2, final_g):
    B, S, D = x.shape
    T = B * S
    depth = xattn_norm_g.shape[0]
    M = mem.shape[1]
    W = rnn_lam_f.shape[-1]

    h = x.reshape(T, D)
    mem2 = mem.reshape(B * M, D)
    rope = rope_tables(positions)
    lam4 = jnp.stack([attn_lambda_q1, attn_lambda_k1, attn_lambda_q2, attn_lambda_k2], axis=1)

    stats = None

    def normed(w3, layer, g2, **kw):
        if stats is None:
            return matmul(rmsnorm(h, g2, layer, BF16), w3, layer, **kw)
        return matmul(stats[0], w3, layer, norm=(g2[layer], stats[1]), **kw)

    nslab = mlp_w2.shape[1] // min(MM_TK, mlp_w2.shape[1])
    for i in range(depth):
        j = i // N_MIXERS
        if i % N_MIXERS == 0:
            qkv = normed(attn_w_qkv, j, attn_norm_g, out_dtype=BF16,
                         scale=DA_HEAD_DIM ** -0.5, rope=rope, rope_width=D)
            o = diff_attention(qkv, lam4, attn_subln_g, j, B=B, S=S, D=D,
                               lam_init=_lambda_init(i))
            h, *stats = matmul(o, attn_w_o, j, out_dtype=F32, res=h, stats_out=True)
        else:
            ug = normed(rnn_w_in, j, rnn_norm_g, out_dtype=F32)
            y = rglru(ug, j, rnn_conv_w, rnn_conv_b,
                      ((rnn_wa_f, rnn_ba_f, rnn_wi_f, rnn_bi_f, rnn_lam_f),
                       (rnn_wa_b, rnn_ba_b, rnn_wi_b, rnn_bi_b, rnn_lam_b)),
                      B=B, S=S, W=W)
            h, *stats = matmul(y, rnn_w_out, j, out_dtype=F32, res=h, stats_out=True)

        q = normed(xattn_w_q, i, xattn_norm_g, out_dtype=BF16, scale=X_HEAD_DIM ** -0.5)
        kv = matmul(rmsnorm(mem2, xattn_mem_g, i, BF16), xattn_w_kv, i, out_dtype=BF16)
        o = xattn_core(q, kv, B=B, S=S)
        h, *stats = matmul(o, xattn_w_o, i, out_dtype=F32, res=h, tm=512, tn=2048,
                           stats_out=True)

        hid = normed(mlp_w1, i, mlp_norm_g, out_dtype=BF16, relu2=True)
        for ks in range(nslab):
            if ks == nslab - 1 and i < depth - 1:
                h, *stats = matmul(hid, mlp_w2, i, out_dtype=F32, res=h, kslab=ks, stats_out=True)
            else:
                h = matmul(hid, mlp_w2, i, out_dtype=F32, res=h, kslab=ks)

    return rmsnorm(h, final_g.reshape(1, D), 0, F32).reshape(B, S, D)
```

```python
import functools
import math

import jax
import jax.numpy as jnp
from jax import lax
from jax.experimental import pallas as pl
from jax.experimental.pallas import tpu as pltpu

F32 = jnp.float32
BF16 = jnp.bfloat16

EPS = 1e-6
ROPE_THETA = 500000.0
DA_HEAD_DIM = 128
ROT_DIM = DA_HEAD_DIM // 4
LRU_BLOCK_W = 256
LRU_C = 8.0
CONV_WIDTH = 4
CONV_LEFT = 2
X_HEADS = 4
X_HEAD_DIM = 128
N_MIXERS = 2

LANES = 128
SUBLANES = 8
VMEM_LIMIT_BYTES = 56 * 1024 * 1024
MM_TK = 4096
EXP_PANEL = 64


def _params(*semantics):
    return pltpu.CompilerParams(dimension_semantics=semantics,
                                vmem_limit_bytes=VMEM_LIMIT_BYTES)


def _rope_table_kernel(pos_ref, invf_ref, cos_ref, sa_ref, sb_ref):
    ang = pos_ref[...].astype(F32) * invf_ref[...]
    lane = lax.broadcasted_iota(jnp.int32, ang.shape, 1)
    half = ROT_DIM // 2
    c, s = jnp.cos(ang), jnp.sin(ang)
    cos_ref[...] = jnp.where(lane < ROT_DIM, c, 1.0)
    sa_ref[...] = jnp.where(lane < half, -s, 0.0)
    sb_ref[...] = jnp.where((lane >= half) & (lane < ROT_DIM), s, 0.0)


def rope_tables(positions, tm=1024):
    T = positions.size
    tm = min(tm, T)
    half = ROT_DIM // 2
    inv_freq = ROPE_THETA ** (-jnp.arange(0, ROT_DIM, 2, dtype=F32) / ROT_DIM)
    invf = jnp.tile(inv_freq, LANES // half)[None, :]
    tab = jax.ShapeDtypeStruct((T, LANES), F32)
    spec = pl.BlockSpec((tm, LANES), lambda i: (i, 0))
    return pl.pallas_call(
        _rope_table_kernel,
        grid=(T // tm,),
        in_specs=[pl.BlockSpec((tm, 1), lambda i: (i, 0)),
                  pl.BlockSpec((1, LANES), lambda i: (0, 0))],
        out_specs=[spec, spec, spec],
        out_shape=[tab, tab, tab],
        compiler_params=_params("parallel"),
        name="rope_tables",
    )(positions.reshape(T, 1), invf)


def _rmsnorm_kernel(x_ref, g_ref, o_ref):
    x = x_ref[...].astype(F32)
    ms = jnp.mean(x * x, axis=-1, keepdims=True)
    o_ref[...] = (x * lax.rsqrt(ms + EPS) * g_ref[...]).astype(o_ref.dtype)


def rmsnorm(x, g2, layer, out_dtype, tm=256):
    T, D = x.shape
    tm = min(tm, T)
    g3 = g2.reshape(g2.shape[0], 1, D)
    return pl.pallas_call(
        _rmsnorm_kernel,
        grid=(T // tm,),
        in_specs=[pl.BlockSpec((tm, D), lambda i: (i, 0)),
                  pl.BlockSpec((None, 1, D), lambda i: (layer, 0, 0))],
        out_specs=pl.BlockSpec((tm, D), lambda i: (i, 0)),
        out_shape=jax.ShapeDtypeStruct((T, D), out_dtype),
        compiler_params=_params("parallel"),
        name="rmsnorm",
    )(x, g3)


def _rope_cols(t, cos, sa, sb):
    half = ROT_DIM // 2
    return (t * cos + pltpu.roll(t, LANES - half, 1) * sa + pltpu.roll(t, half, 1) * sb)


def _mm_kernel(*refs, relu2, scale, has_res, rope_tiles, norm_dim, stats_out):
    it = iter(refs)
    a_ref, w_ref = next(it), next(it)
    res_ref = next(it) if has_res else None
    if rope_tiles:
        cos_ref, sa_ref, sb_ref = next(it), next(it), next(it)
    if norm_dim:
        g_ref, ssq_ref = next(it), next(it)
    o_ref = next(it)
    if stats_out:
        ob_ref, osq_ref = next(it), next(it)
    wbf_ref = next(it)
    if stats_out:
        acc_sq_ref = next(it)
    j, i = pl.program_id(0), pl.program_id(1)

    if stats_out:
        @pl.when(j == 0)
        def _():
            acc_sq_ref[i] = jnp.zeros(acc_sq_ref.shape[1:], F32)

    def tile(wb):
        acc = jnp.dot(a_ref[...], wb, preferred_element_type=F32)
        if norm_dim:
            ssq = jnp.sum(ssq_ref[...], axis=1, keepdims=True)
            acc = acc * lax.rsqrt(ssq * (1.0 / norm_dim) + EPS)

        if rope_tiles:
            rotary = j < 2 * rope_tiles
            qs = jnp.where(j < rope_tiles, scale, 1.0).astype(F32)
            cos = jnp.where(rotary, cos_ref[...], 1.0) * qs
            sa = jnp.where(rotary, sa_ref[...], 0.0) * qs
            sb = jnp.where(rotary, sb_ref[...], 0.0) * qs
            for c in range(acc.shape[1] // LANES):
                t = _rope_cols(acc[:, c * LANES:(c + 1) * LANES], cos, sa, sb)
                o_ref[:, c * LANES:(c + 1) * LANES] = t.astype(o_ref.dtype)
            return
        if relu2:
            r = jnp.maximum(acc, 0.0)
            acc = r * r
        if scale is not None:
            acc = acc * scale
        if has_res:
            acc = acc + res_ref[...]
        o_ref[...] = acc.astype(o_ref.dtype)
        if stats_out:
            ob_ref[...] = acc.astype(BF16)
            sq = acc * acc
            part = sq[:, 0:LANES]
            for c in range(1, acc.shape[1] // LANES):
                part = part + sq[:, c * LANES:(c + 1) * LANES]
            total = acc_sq_ref[i] + part
            acc_sq_ref[i] = total
            osq_ref[...] = total

    @pl.when(i == 0)
    def _():
        w = w_ref[...]
        if norm_dim:
            w = w * g_ref[...]
        wb = w.astype(BF16)
        wbf_ref[...] = wb
        tile(wb)

    @pl.when(i > 0)
    def _():
        tile(wbf_ref[...])


def matmul(a, w3, layer, *, out_dtype, kslab=0, tm=1024, tn=512, tk=MM_TK, relu2=False,
           scale=None, res=None, rope=None, rope_width=None, norm=None, stats_out=False):
    M = a.shape[0]
    _, K, N = w3.shape
    tm, tn, tk = min(tm, M), min(tn, N), min(tk, K)
    nj, ni = N // tn, M // tm
    in_specs = [pl.BlockSpec((tm, tk), lambda j, i: (i, kslab)),
                pl.BlockSpec((None, tk, tn), lambda j, i: (layer, kslab, j))]
    args = [a, w3]
    if res is not None:
        in_specs.append(pl.BlockSpec((tm, tn), lambda j, i: (i, j)))
        args.append(res)
    rope_tiles = 0
    if rope is not None:
        assert rope_width % tn == 0
        rope_tiles = rope_width // tn
        in_specs += [pl.BlockSpec((tm, LANES), lambda j, i: (i, 0))] * 3
        args += list(rope)
    if norm is not None:
        assert tk == K, "the folded rmsnorm needs the whole row in one contraction slab"
        g, ssq = norm
        in_specs += [pl.BlockSpec((tk, 1), lambda j, i: (0, 0)),
                     pl.BlockSpec((tm, LANES), lambda j, i: (i, 0))]
        args += [g.reshape(K, 1), ssq]
    out_specs = pl.BlockSpec((tm, tn), lambda j, i: (i, j))
    out_shape = jax.ShapeDtypeStruct((M, N), out_dtype)
    scratch = [pltpu.VMEM((tk, tn), BF16)]
    if stats_out:
        out_specs = [out_specs, pl.BlockSpec((tm, tn), lambda j, i: (i, j)),
                     pl.BlockSpec((tm, LANES), lambda j, i: (jnp.where(j == nj - 1, i, 0), 0))]
        out_shape = [out_shape, jax.ShapeDtypeStruct((M, N), BF16),
                     jax.ShapeDtypeStruct((M, LANES), F32)]
        scratch.append(pltpu.VMEM((ni, tm, LANES), F32))
    kern = functools.partial(_mm_kernel, relu2=relu2, scale=scale, has_res=res is not None,
                             rope_tiles=rope_tiles, norm_dim=K if norm is not None else 0,
                             stats_out=stats_out)
    return pl.pallas_call(
        kern,
        grid=(nj, ni),
        in_specs=in_specs,
        out_specs=out_specs,
        out_shape=out_shape,
        scratch_shapes=scratch,
        compiler_params=_params("arbitrary", "arbitrary"),
        name="matmul",
    )(*args)


def _dattn_kernel(lam_ref, q_ref, k_ref, v_ref, g_ref, o_ref, s_ref, p_ref, *, kc, rb, lam_init):
    tq = q_ref.shape[0]
    nkc = k_ref.shape[0] // kc
    nrb = tq // rb
    npan = rb // EXP_PANEL
    d = DA_HEAD_DIM
    lv = lam_ref[...]
    lam = (jnp.exp(jnp.sum(lv[0:1] * lv[1:2], axis=1, keepdims=True))
           - jnp.exp(jnp.sum(lv[2:3] * lv[3:4], axis=1, keepdims=True)) + lam_init)
    maps = (0, 1)

    def scores(r, c, mp):
        for m in maps:
            kk = k_ref[c * kc:(c + 1) * kc, m * d:(m + 1) * d]
            s = lax.dot_general(q_ref[r * rb:(r + 1) * rb, m * d:(m + 1) * d], kk,
                                (((1,), (1,)), ((), ())), preferred_element_type=F32)
            s_ref[m, c] = s
            cm = s[:, 0:LANES]
            for t in range(1, kc // LANES):
                cm = jnp.maximum(cm, s[:, t * LANES:(t + 1) * LANES])
            mp[m] = cm if mp[m] is None else jnp.maximum(mp[m], cm)

    def exps(c, mb, lp):
        for m in maps:
            for i in range(rb // EXP_PANEL):
                rows = slice(i * EXP_PANEL, (i + 1) * EXP_PANEL)
                cl = None
                for t in range(kc // LANES):
                    pt = jnp.exp(s_ref[m, c, rows, t * LANES:(t + 1) * LANES] - mb[m][i])
                    cl = pt if cl is None else cl + pt
                    p_ref[m, rows, c * kc + t * LANES:c * kc + (t + 1) * LANES] = pt.astype(BF16)
                lp[m][i] = cl if lp[m][i] is None else lp[m][i] + cl

    def weighted(r, lp):
        hk = k_ref.shape[0] // 2
        heads = []
        for m in maps:
            pv = (jnp.dot(p_ref[m, :, 0:hk], v_ref[0:hk, :], preferred_element_type=F32)
                  + jnp.dot(p_ref[m, :, hk:], v_ref[hk:, :], preferred_element_type=F32))
            l = jnp.concatenate([jnp.sum(x, axis=1, keepdims=True) for x in lp[m]], axis=0)
            heads.append(pv / l)
        o = heads[0] - lam * heads[1]
        ms = jnp.mean(o * o, axis=1, keepdims=True)
        o = o * lax.rsqrt(ms + EPS) * g_ref[...] * (1.0 - lam_init)
        o_ref[r * rb:(r + 1) * rb, :] = o.astype(o_ref.dtype)

    mp, mb, lp = {}, {}, {}
    for stage in range(nrb + 2):
        ra, re, rp = stage, stage - 1, stage - 2
        if 0 <= rp < nrb:
            weighted(rp, lp[rp])
        if 0 <= ra < nrb:
            mp[ra] = [None, None]
        if 0 <= re < nrb:
            lp[re] = [[None] * npan for _ in maps]
        for c in range(nkc):
            if 0 <= re < nrb:
                exps(c, mb[re], lp[re])
            if 0 <= ra < nrb:
                scores(ra, c, mp[ra])
        if 0 <= ra < nrb:
            mb[ra] = []
            for m in maps:
                rowmax = jnp.max(mp[ra][m], axis=1, keepdims=True)
                mb[ra].append([jnp.broadcast_to(rowmax[i * EXP_PANEL:(i + 1) * EXP_PANEL],
                                                (EXP_PANEL, LANES)) for i in range(npan)])


def diff_attention(qkv, lam4, subln_g, layer, *, B, S, D, lam_init, tq=1024, rb=512, kc=512):
    T = B * S
    hw = 2 * DA_HEAD_DIM
    H = D // hw
    tq = min(tq, S)
    rb = min(rb, tq)
    nq = S // tq
    g3 = subln_g.reshape(subln_g.shape[0], 1, hw)
    kern = functools.partial(_dattn_kernel, kc=kc, rb=rb, lam_init=lam_init)
    return pl.pallas_call(
        kern,
        grid=(B, H, nq),
        in_specs=[pl.BlockSpec((None, 4, DA_HEAD_DIM), lambda b, h, i: (layer, 0, 0)),
                  pl.BlockSpec((tq, hw), lambda b, h, i: (b * nq + i, h)),
                  pl.BlockSpec((S, hw), lambda b, h, i: (b, H + h)),
                  pl.BlockSpec((S, hw), lambda b, h, i: (b, 2 * H + h)),
                  pl.BlockSpec((None, 1, hw), lambda b, h, i: (layer, 0, 0))],
        out_specs=pl.BlockSpec((tq, hw), lambda b, h, i: (b * nq + i, h)),
        out_shape=jax.ShapeDtypeStruct((T, D), BF16),
        scratch_shapes=[pltpu.VMEM((2, S // kc, rb, kc), F32),
                        pltpu.VMEM((2, rb, S), BF16)],
        compiler_params=_params("parallel", "parallel", "arbitrary"),
        name="diff_attention",
    )(lam4, qkv, qkv, qkv, g3)


def _xattn_kernel(q_ref, k_ref, v_ref, o_ref):
    hd = X_HEAD_DIM
    for h in range(X_HEADS):
        q = q_ref[:, h * hd:(h + 1) * hd]
        k = k_ref[:, h * hd:(h + 1) * hd]
        v = v_ref[:, h * hd:(h + 1) * hd]
        s = lax.dot_general(q, k, (((1,), (1,)), ((), ())), preferred_element_type=F32)
        p = jnp.exp(s - jnp.max(s, axis=1, keepdims=True))
        p = p / jnp.sum(p, axis=1, keepdims=True)
        o = jnp.dot(p.astype(BF16), v, preferred_element_type=F32)
        o_ref[:, h * hd:(h + 1) * hd] = o.astype(o_ref.dtype)


def xattn_core(q, kv, *, B, S, tm=512):
    T, XW = q.shape
    M = kv.shape[0] // B
    nt = S // tm
    return pl.pallas_call(
        _xattn_kernel,
        grid=(B, nt),
        in_specs=[pl.BlockSpec((tm, XW), lambda b, i: (b * nt + i, 0)),
                  pl.BlockSpec((M, XW), lambda b, i: (b, 0)),
                  pl.BlockSpec((M, XW), lambda b, i: (b, 1))],
        out_specs=pl.BlockSpec((tm, XW), lambda b, i: (b * nt + i, 0)),
        out_shape=jax.ShapeDtypeStruct((T, XW), BF16),
        compiler_params=_params("parallel", "parallel"),
        name="xattn_core",
    )(q, kv, kv)


def _softplus(x):
    return jnp.maximum(x, 0.0) + jnp.log1p(jnp.exp(-jnp.abs(x)))


def _sigmoid(x):
    return 1.0 / (1.0 + jnp.exp(-x))


def _gelu_tanh(g):
    return 0.5 * g * (1.0 + jnp.tanh(math.sqrt(2.0 / math.pi) * (g + 0.044715 * (g * g * g))))


def _blocked_scan(a_ref, x_ref, h_ref, gp_ref, gh_ref, ent_ref, l3_ref, d3_ref, e3_ref,
                  h0, reverse):
    tc = a_ref.shape[0]
    g = tc // SUBLANES
    n3 = g // SUBLANES
    order = tuple(range(SUBLANES - 1, -1, -1)) if reverse else tuple(range(SUBLANES))
    korder = tuple(range(n3 - 1, -1, -1)) if reverse else tuple(range(n3))
    last = order[-1]

    def strided(ref, r, n):
        return ref[pl.ds(r, n, stride=SUBLANES), :]

    loc, dec = {}, {}
    prev = None
    for r in order:
        a, x = strided(a_ref, r, g), strided(x_ref, r, g)
        loc[r] = x if prev is None else a * loc[prev] + x
        dec[r] = a if prev is None else a * dec[prev]
        prev = r
    gp_ref[...] = dec[last]
    gh_ref[...] = loc[last]

    loc2, dec2 = {}, {}
    prev = None
    for s in order:
        q, k = strided(gp_ref, s, n3), strided(gh_ref, s, n3)
        loc2[s] = k if prev is None else q * loc2[prev] + k
        dec2[s] = q if prev is None else q * dec2[prev]
        prev = s

    state = h0
    l3_ref[...] = loc2[last]
    d3_ref[...] = dec2[last]
    for k in korder:
        e3_ref[k:k + 1, :] = state
        state = l3_ref[k:k + 1, :] + d3_ref[k:k + 1, :] * state
    entering3 = e3_ref[...]

    entering = entering3
    for s in order:
        ent_ref[pl.ds(s, n3, stride=SUBLANES), :] = entering
        entering = loc2[s] + dec2[s] * entering3
    ent = ent_ref[...]
    for r in order:
        h_ref[pl.ds(r, g, stride=SUBLANES), :] = loc[r] + dec[r] * ent
    return state


def _lru_kernel(u_ref, up_ref, un_ref, gate_ref, cw_ref, cb_ref,
                waf_ref, baf_ref, wif_ref, bif_ref, lamf_ref,
                wab_ref, bab_ref, wib_ref, bib_ref, lamb_ref,
                y_ref, pad_ref, a_ref, x_ref, h_ref, gp_ref, gh_ref, ent_ref,
                l3_ref, d3_ref, e3_ref, hf_ref, uc_ref, st_ref):
    ph = pl.program_id(2)
    c = pl.program_id(3)
    nt = pl.num_programs(3)
    cc = jnp.where(ph == 0, c, nt - 1 - c)
    tc, bw = u_ref.shape
    h = SUBLANES
    row0 = pl.multiple_of(cc * tc, tc)
    lane_tiles = [slice(l * LANES, (l + 1) * LANES) for l in range(bw // LANES)]

    @pl.when(c == 0)
    def _():
        st_ref[...] = jnp.zeros_like(st_ref)

    def conv():
        pad_ref[0:h] = jnp.where(cc > 0, up_ref[...], 0.0)
        pad_ref[h:h + tc] = u_ref[...]
        pad_ref[h + tc:2 * h + tc] = jnp.where(cc < nt - 1, un_ref[...], 0.0)
        cw = cw_ref[...]
        uc = cb_ref[...]
        for t in range(CONV_WIDTH):
            off = h + t - CONV_LEFT
            uc = uc + pad_ref[off:off + tc] * cw[t:t + 1]
        return uc

    def sweep(uc, wa, ba, wi, bi, lam, reverse):
        ub = uc.astype(BF16)

        def gate(w_ref, b_ref):
            z = jnp.dot(ub, w_ref[...].astype(BF16), preferred_element_type=F32) + b_ref[...]
            return _sigmoid(z)

        r = gate(wa, ba)
        ig = gate(wi, bi)
        log_a = (-LRU_C) * r * _softplus(-lam[...])
        a = jnp.exp(log_a)
        x = jnp.sqrt(jnp.tanh(-log_a) * (1.0 + a * a)) * (ig * uc)
        for l, lanes in enumerate(lane_tiles):
            a_ref[l] = a[:, lanes]
            x_ref[l] = x[:, lanes]
        for l, lanes in enumerate(lane_tiles):
            st_ref[0:1, lanes] = _blocked_scan(
                a_ref.at[l], x_ref.at[l], h_ref.at[l], gp_ref.at[l], gh_ref.at[l], ent_ref.at[l],
                l3_ref.at[l], d3_ref.at[l], e3_ref.at[l], st_ref[0:1, lanes], reverse)

    @pl.when(ph == 0)
    def _():
        uc = conv()
        uc_ref[pl.ds(row0, tc), :] = uc
        sweep(uc, waf_ref, baf_ref, wif_ref, bif_ref, lamf_ref, False)
        for l in range(len(lane_tiles)):
            hf_ref[l, pl.ds(row0, tc), :] = h_ref[l]

    @pl.when(ph == 1)
    def _():
        sweep(uc_ref[pl.ds(row0, tc), :], wab_ref, bab_ref, wib_ref, bib_ref, lamb_ref, True)
        for l, lanes in enumerate(lane_tiles):
            hsum = hf_ref[l, pl.ds(row0, tc), :] + h_ref[l]
            y_ref[:, lanes] = (hsum * _gelu_tanh(gate_ref[:, lanes])).astype(y_ref.dtype)


def rglru(ug, layer, conv_w, conv_b, dirs, *, B, S, W, tc=1024):
    T = B * S
    bw = LRU_BLOCK_W
    nb = W // bw
    tc = min(tc, S)
    nt = S // tc
    h = SUBLANES
    assert tc % (h * h) == 0 and S % tc == 0
    last8 = T // h - 1

    def early(p, c):
        return c + p * (nt - 1 - c)

    def late(p, c):
        return nt - 1 - p * c

    in_specs = [
        pl.BlockSpec((tc, bw), lambda n, b, p, c: (b * nt + early(p, c), n)),
        pl.BlockSpec((h, bw), lambda n, b, p, c: (
            jnp.maximum((b * S + early(p, c) * tc) // h - 1, 0), n)),
        pl.BlockSpec((h, bw), lambda n, b, p, c: (
            jnp.minimum((b * S + (early(p, c) + 1) * tc) // h, last8), n)),
        pl.BlockSpec((tc, bw), lambda n, b, p, c: (b * nt + late(p, c), nb + n)),
        pl.BlockSpec((None, CONV_WIDTH, bw), lambda n, b, p, c: (layer, 0, n)),
        pl.BlockSpec((None, 1, bw), lambda n, b, p, c: (layer, 0, n))]
    args = [ug, ug, ug, ug, conv_w, conv_b.reshape(conv_b.shape[0], 1, W)]
    wspec = pl.BlockSpec((None, None, bw, bw), lambda n, b, p, c: (layer, n, 0, 0))
    vspec = pl.BlockSpec((None, None, 1, bw), lambda n, b, p, c: (layer, n, 0, 0))
    for wa, ba, wi, bi, lam in dirs:
        L = wa.shape[0]
        in_specs += [wspec, vspec, wspec, vspec, vspec]
        args += [wa, ba.reshape(L, nb, 1, bw), wi, bi.reshape(L, nb, 1, bw),
                 lam.reshape(L, nb, 1, bw)]
    g = tc // h
    nl = bw // LANES

    def tiles(rows):
        return pltpu.VMEM((nl, rows, LANES), F32)

    return pl.pallas_call(
        _lru_kernel,
        grid=(nb, B, 2, nt),
        in_specs=in_specs,
        out_specs=pl.BlockSpec((tc, bw), lambda n, b, p, c: (b * nt + late(p, c), n)),
        out_shape=jax.ShapeDtypeStruct((T, W), BF16),
        scratch_shapes=[pltpu.VMEM((tc + 2 * h, bw), F32),
                        tiles(tc),
                        tiles(tc),
                        tiles(tc),
                        tiles(g),
                        tiles(g),
                        tiles(g),
                        tiles(g // h),
                        tiles(g // h),
                        tiles(g // h),
                        tiles(S),
                        pltpu.VMEM((S, bw), F32),
                        pltpu.VMEM((h, bw), F32)],
        compiler_params=_params("parallel", "parallel", "arbitrary", "arbitrary"),
        name="rglru",
    )(*args)


def _lambda_init(layer_idx):
    return 0.8 - 0.6 * math.exp(-0.3 * layer_idx)


def kernel(x, mem, positions, attn_norm_g, attn_w_qkv, attn_lambda_q1, attn_lambda_k1, attn_lambda_q2, attn_lambda_k2, attn_subln_g, attn_w_o, rnn_norm_g, rnn_w_in, rnn_conv_w, rnn_conv_b, rnn_wa_f, rnn_ba_f, rnn_wi_f, rnn_bi_f, rnn_lam_f, rnn_wa_b, rnn_ba_b, rnn_wi_b, rnn_bi_b, rnn_lam_b, rnn_w_out, xattn_norm_g, xattn_mem_g, xattn_w_q, xattn_w_kv, xattn_w_o, mlp_norm_g, mlp_w1, mlp_w2, final_g):
    B, S, D = x.shape
    T = B * S
    depth = xattn_norm_g.shape[0]
    M = mem.shape[1]
    W = rnn_lam_f.shape[-1]

    h = x.reshape(T, D)
    mem2 = mem.reshape(B * M, D)
    rope = rope_tables(positions)
    lam4 = jnp.stack([attn_lambda_q1, attn_lambda_k1, attn_lambda_q2, attn_lambda_k2], axis=1)

    stats = None

    def normed(w3, layer, g2, **kw):
        if stats is None:
            return matmul(rmsnorm(h, g2, layer, BF16), w3, layer, **kw)
        return matmul(stats[0], w3, layer, norm=(g2[layer], stats[1]), **kw)

    nslab = mlp_w2.shape[1] // min(MM_TK, mlp_w2.shape[1])
    for i in range(depth):
        j = i // N_MIXERS
        if i % N_MIXERS == 0:
            qkv = normed(attn_w_qkv, j, attn_norm_g, out_dtype=BF16,
                         scale=DA_HEAD_DIM ** -0.5, rope=rope, rope_width=D)
            o = diff_attention(qkv, lam4, attn_subln_g, j, B=B, S=S, D=D,
                               lam_init=_lambda_init(i))
            h, *stats = matmul(o, attn_w_o, j, out_dtype=F32, res=h, stats_out=True)
        else:
            ug = normed(rnn_w_in, j, rnn_norm_g, out_dtype=F32)
            y = rglru(ug, j, rnn_conv_w, rnn_conv_b,
                      ((rnn_wa_f, rnn_ba_f, rnn_wi_f, rnn_bi_f, rnn_lam_f),
                       (rnn_wa_b, rnn_ba_b, rnn_wi_b, rnn_bi_b, rnn_lam_b)),
                      B=B, S=S, W=W)
            h, *stats = matmul(y, rnn_w_out, j, out_dtype=F32, res=h, stats_out=True)

        q = normed(xattn_w_q, i, xattn_norm_g, out_dtype=BF16, scale=X_HEAD_DIM ** -0.5)
        kv = matmul(rmsnorm(mem2, xattn_mem_g, i, BF16), xattn_w_kv, i, out_dtype=BF16)
        o = xattn_core(q, kv, B=B, S=S)
        h, *stats = matmul(o, xattn_w_o, i, out_dtype=F32, res=h, tm=512, tn=2048,
                           stats_out=True)

        hid = normed(mlp_w1, i, mlp_norm_g, out_dtype=BF16, relu2=True)
        for ks in range(nslab):
            if ks == nslab - 1 and i < depth - 1:
                h, *stats = matmul(hid, mlp_w2, i, out_dtype=F32, res=h, kslab=ks, stats_out=True)
            else:
                h = matmul(hid, mlp_w2, i, out_dtype=F32, res=h, kslab=ks)

    return rmsnorm(h, final_g.reshape(1, D), 0, F32).reshape(B, S, D)
```

```python
import functools
import math

import jax
import jax.numpy as jnp
from jax import lax
from jax.experimental import pallas as pl
from jax.experimental.pallas import tpu as pltpu

F32 = jnp.float32
BF16 = jnp.bfloat16

EPS = 1e-6
ROPE_THETA = 500000.0
DA_HEAD_DIM = 128
ROT_DIM = DA_HEAD_DIM // 4
LRU_BLOCK_W = 256
LRU_C = 8.0
CONV_WIDTH = 4
CONV_LEFT = 2
X_HEADS = 4
X_HEAD_DIM = 128
N_MIXERS = 2

LANES = 128
SUBLANES = 8
VMEM_LIMIT_BYTES = 56 * 1024 * 1024
MM_TK = 4096
EXP_PANEL = 64


def _params(*semantics):
    return pltpu.CompilerParams(dimension_semantics=semantics,
                                vmem_limit_bytes=VMEM_LIMIT_BYTES)


def _rope_table_kernel(pos_ref, invf_ref, cos_ref, sa_ref, sb_ref):
    ang = pos_ref[...].astype(F32) * invf_ref[...]
    lane = lax.broadcasted_iota(jnp.int32, ang.shape, 1)
    half = ROT_DIM // 2
    c, s = jnp.cos(ang), jnp.sin(ang)
    cos_ref[...] = jnp.where(lane < ROT_DIM, c, 1.0)
    sa_ref[...] = jnp.where(lane < half, -s, 0.0)
    sb_ref[...] = jnp.where((lane >= half) & (lane < ROT_DIM), s, 0.0)


def rope_tables(positions, tm=1024):
    T = positions.size
    tm = min(tm, T)
    half = ROT_DIM // 2
    inv_freq = ROPE_THETA ** (-jnp.arange(0, ROT_DIM, 2, dtype=F32) / ROT_DIM)
    invf = jnp.tile(inv_freq, LANES // half)[None, :]
    tab = jax.ShapeDtypeStruct((T, LANES), F32)
    spec = pl.BlockSpec((tm, LANES), lambda i: (i, 0))
    return pl.pallas_call(
        _rope_table_kernel,
        grid=(T // tm,),
        in_specs=[pl.BlockSpec((tm, 1), lambda i: (i, 0)),
                  pl.BlockSpec((1, LANES), lambda i: (0, 0))],
        out_specs=[spec, spec, spec],
        out_shape=[tab, tab, tab],
        compiler_params=_params("parallel"),
        name="rope_tables",
    )(positions.reshape(T, 1), invf)


def _rmsnorm_kernel(x_ref, g_ref, o_ref):
    x = x_ref[...].astype(F32)
    ms = jnp.mean(x * x, axis=-1, keepdims=True)
    o_ref[...] = (x * lax.rsqrt(ms + EPS) * g_ref[...]).astype(o_ref.dtype)


def rmsnorm(x, g2, layer, out_dtype, tm=256):
    T, D = x.shape
    tm = min(tm, T)
    g3 = g2.reshape(g2.shape[0], 1, D)
    return pl.pallas_call(
        _rmsnorm_kernel,
        grid=(T // tm,),
        in_specs=[pl.BlockSpec((tm, D), lambda i: (i, 0)),
                  pl.BlockSpec((None, 1, D), lambda i: (layer, 0, 0))],
        out_specs=pl.BlockSpec((tm, D), lambda i: (i, 0)),
        out_shape=jax.ShapeDtypeStruct((T, D), out_dtype),
        compiler_params=_params("parallel"),
        name="rmsnorm",
    )(x, g3)


def _rope_cols(t, cos, sa, sb):
    half = ROT_DIM // 2
    return (t * cos + pltpu.roll(t, LANES - half, 1) * sa + pltpu.roll(t, half, 1) * sb)


def _mm_kernel(*refs, relu2, scale, has_res, rope_tiles, norm_dim, stats_out):
    it = iter(refs)
    a_ref, w_ref = next(it), next(it)
    res_ref = next(it) if has_res else None
    if rope_tiles:
        cos_ref, sa_ref, sb_ref = next(it), next(it), next(it)
    if norm_dim:
        g_ref, ssq_ref = next(it), next(it)
    o_ref = next(it)
    if stats_out:
        ob_ref, osq_ref = next(it), next(it)
    wbf_ref = next(it)
    if stats_out:
        acc_sq_ref = next(it)
    j, i = pl.program_id(0), pl.program_id(1)

    if stats_out:
        @pl.when(j == 0)
        def _():
            acc_sq_ref[i] = jnp.zeros(acc_sq_ref.shape[1:], F32)

    def tile(wb):
        acc = jnp.dot(a_ref[...], wb, preferred_element_type=F32)
        if norm_dim:
            ssq = jnp.sum(ssq_ref[...], axis=1, keepdims=True)
            acc = acc * lax.rsqrt(ssq * (1.0 / norm_dim) + EPS)

        if rope_tiles:
            rotary = j < 2 * rope_tiles
            qs = jnp.where(j < rope_tiles, scale, 1.0).astype(F32)
            cos = jnp.where(rotary, cos_ref[...], 1.0) * qs
            sa = jnp.where(rotary, sa_ref[...], 0.0) * qs
            sb = jnp.where(rotary, sb_ref[...], 0.0) * qs
            for c in range(acc.shape[1] // LANES):
                t = _rope_cols(acc[:, c * LANES:(c + 1) * LANES], cos, sa, sb)
                o_ref[:, c * LANES:(c + 1) * LANES] = t.astype(o_ref.dtype)
            return
        if relu2:
            r = jnp.maximum(acc, 0.0)
            acc = r * r
        if scale is not None:
            acc = acc * scale
        if has_res:
            acc = acc + res_ref[...]
        o_ref[...] = acc.astype(o_ref.dtype)
        if stats_out:
            ob_ref[...] = acc.astype(BF16)
            sq = acc * acc
            part = sq[:, 0:LANES]
            for c in range(1, acc.shape[1] // LANES):
                part = part + sq[:, c * LANES:(c + 1) * LANES]
            total = acc_sq_ref[i] + part
            acc_sq_ref[i] = total
            osq_ref[...] = total

    @pl.when(i == 0)
    def _():
        w = w_ref[...]
        if norm_dim:
            w = w * g_ref[...]
        wb = w.astype(BF16)
        wbf_ref[...] = wb
        tile(wb)

    @pl.when(i > 0)
    def _():
        tile(wbf_ref[...])


def matmul(a, w3, layer, *, out_dtype, kslab=0, tm=1024, tn=512, tk=MM_TK, relu2=False,
           scale=None, res=None, rope=None, rope_width=None, norm=None, stats_out=False):
    M = a.shape[0]
    _, K, N = w3.shape
    tm, tn, tk = min(tm, M), min(tn, N), min(tk, K)
    nj, ni = N // tn, M // tm
    in_specs = [pl.BlockSpec((tm, tk), lambda j, i: (i, kslab)),
                pl.BlockSpec((None, tk, tn), lambda j, i: (layer, kslab, j))]
    args = [a, w3]
    if res is not None:
        in_specs.append(pl.BlockSpec((tm, tn), lambda j, i: (i, j)))
        args.append(res)
    rope_tiles = 0
    if rope is not None:
        assert rope_width % tn == 0
        rope_tiles = rope_width // tn
        in_specs += [pl.BlockSpec((tm, LANES), lambda j, i: (i, 0))] * 3
        args += list(rope)
    if norm is not None:
        assert tk == K, "the folded rmsnorm needs the whole row in one contraction slab"
        g, ssq = norm
        in_specs += [pl.BlockSpec((tk, 1), lambda j, i: (0, 0)),
                     pl.BlockSpec((tm, LANES), lambda j, i: (i, 0))]
        args += [g.reshape(K, 1), ssq]
    out_specs = pl.BlockSpec((tm, tn), lambda j, i: (i, j))
    out_shape = jax.ShapeDtypeStruct((M, N), out_dtype)
    scratch = [pltpu.VMEM((tk, tn), BF16)]
    if stats_out:
        out_specs = [out_specs, pl.BlockSpec((tm, tn), lambda j, i: (i, j)),
                     pl.BlockSpec((tm, LANES), lambda j, i: (jnp.where(j == nj - 1, i, 0), 0))]
        out_shape = [out_shape, jax.ShapeDtypeStruct((M, N), BF16),
                     jax.ShapeDtypeStruct((M, LANES), F32)]
        scratch.append(pltpu.VMEM((ni, tm, LANES), F32))
    kern = functools.partial(_mm_kernel, relu2=relu2, scale=scale, has_res=res is not None,
                             rope_tiles=rope_tiles, norm_dim=K if norm is not None else 0,
                             stats_out=stats_out)
    return pl.pallas_call(
        kern,
        grid=(nj, ni),
        in_specs=in_specs,
        out_specs=out_specs,
        out_shape=out_shape,
        scratch_shapes=scratch,
        compiler_params=_params("arbitrary", "arbitrary"),
        name="matmul",
    )(*args)


def _dattn_kernel(lam_ref, q_ref, k_ref, v_ref, g_ref, o_ref, *, kc, rb, lam_init):
    tq = q_ref.shape[0]
    nkc = k_ref.shape[0] // kc
    nrb = tq // rb
    npan = rb // EXP_PANEL
    d = DA_HEAD_DIM
    lv = lam_ref[...]
    lam = (jnp.exp(jnp.sum(lv[0:1] * lv[1:2], axis=1, keepdims=True))
           - jnp.exp(jnp.sum(lv[2:3] * lv[3:4], axis=1, keepdims=True)) + lam_init)
    maps = (0, 1)

    def scores(r, c, mp, sv):
        for m in maps:
            kk = k_ref[c * kc:(c + 1) * kc, m * d:(m + 1) * d]
            s = lax.dot_general(q_ref[r * rb:(r + 1) * rb, m * d:(m + 1) * d], kk,
                                (((1,), (1,)), ((), ())), preferred_element_type=F32)
            sv[m, c] = s
            cm = s[:, 0:LANES]
            for t in range(1, kc // LANES):
                cm = jnp.maximum(cm, s[:, t * LANES:(t + 1) * LANES])
            mp[m] = cm if mp[m] is None else jnp.maximum(mp[m], cm)

    def exps(c, mb, lp, sv, pv):
        for m in maps:
            for i in range(npan):
                rows = slice(i * EXP_PANEL, (i + 1) * EXP_PANEL)
                cl = None
                for t in range(kc // LANES):
                    pt = jnp.exp(sv[m, c][rows, t * LANES:(t + 1) * LANES] - mb[m][i])
                    cl = pt if cl is None else cl + pt
                    pv[m, c, i, t] = pt.astype(BF16)
                lp[m][i] = cl if lp[m][i] is None else lp[m][i] + cl

    def weighted(r, lp, pv):
        hk = k_ref.shape[0] // 2
        heads = []
        for m in maps:
            p = jnp.concatenate(
                [jnp.concatenate(
                    [jnp.concatenate([pv[m, c, i, t] for t in range(kc // LANES)], axis=1)
                     for i in range(npan)], axis=0) for c in range(nkc)], axis=1)
            pvm = (jnp.dot(p[:, 0:hk], v_ref[0:hk, :], preferred_element_type=F32)
                   + jnp.dot(p[:, hk:], v_ref[hk:, :], preferred_element_type=F32))
            l = jnp.concatenate([jnp.sum(x, axis=1, keepdims=True) for x in lp[m]], axis=0)
            heads.append(pvm / l)
        o = heads[0] - lam * heads[1]
        ms = jnp.mean(o * o, axis=1, keepdims=True)
        o = o * lax.rsqrt(ms + EPS) * g_ref[...] * (1.0 - lam_init)
        o_ref[r * rb:(r + 1) * rb, :] = o.astype(o_ref.dtype)

    mp, mb, lp, sv, pv = {}, {}, {}, {}, {}
    for stage in range(nrb + 2):
        ra, re, rp = stage, stage - 1, stage - 2
        if 0 <= rp < nrb:
            weighted(rp, lp[rp], pv[rp])
        if 0 <= ra < nrb:
            mp[ra] = [None, None]
            sv[ra] = {}
        if 0 <= re < nrb:
            lp[re] = [[None] * npan for _ in maps]
            pv[re] = {}
        for c in range(nkc):
            if 0 <= re < nrb:
                exps(c, mb[re], lp[re], sv[re], pv[re])
            if 0 <= ra < nrb:
                scores(ra, c, mp[ra], sv[ra])
        if 0 <= ra < nrb:
            mb[ra] = []
            for m in maps:
                rowmax = jnp.max(mp[ra][m], axis=1, keepdims=True)
                mb[ra].append([jnp.broadcast_to(rowmax[i * EXP_PANEL:(i + 1) * EXP_PANEL],
                                                (EXP_PANEL, LANES)) for i in range(npan)])


def diff_attention(qkv, lam4, subln_g, layer, *, B, S, D, lam_init, tq=1024, rb=512, kc=512):
    T = B * S
    hw = 2 * DA_HEAD_DIM
    H = D // hw
    tq = min(tq, S)
    rb = min(rb, tq)
    nq = S // tq
    g3 = subln_g.reshape(subln_g.shape[0], 1, hw)
    kern = functools.partial(_dattn_kernel, kc=kc, rb=rb, lam_init=lam_init)
    return pl.pallas_call(
        kern,
        grid=(B, H, nq),
        in_specs=[pl.BlockSpec((None, 4, DA_HEAD_DIM), lambda b, h, i: (layer, 0, 0)),
                  pl.BlockSpec((tq, hw), lambda b, h, i: (b * nq + i, h)),
                  pl.BlockSpec((S, hw), lambda b, h, i: (b, H + h)),
                  pl.BlockSpec((S, hw), lambda b, h, i: (b, 2 * H + h)),
                  pl.BlockSpec((None, 1, hw), lambda b, h, i: (layer, 0, 0))],
        out_specs=pl.BlockSpec((tq, hw), lambda b, h, i: (b * nq + i, h)),
        out_shape=jax.ShapeDtypeStruct((T, D), BF16),
        compiler_params=_params("parallel", "parallel", "arbitrary"),
        name="diff_attention",
    )(lam4, qkv, qkv, qkv, g3)


def _xattn_kernel(q_ref, k_ref, v_ref, o_ref):
    hd = X_HEAD_DIM
    for h in range(X_HEADS):
        q = q_ref[:, h * hd:(h + 1) * hd]
        k = k_ref[:, h * hd:(h + 1) * hd]
        v = v_ref[:, h * hd:(h + 1) * hd]
        s = lax.dot_general(q, k, (((1,), (1,)), ((), ())), preferred_element_type=F32)
        p = jnp.exp(s - jnp.max(s, axis=1, keepdims=True))
        p = p / jnp.sum(p, axis=1, keepdims=True)
        o = jnp.dot(p.astype(BF16), v, preferred_element_type=F32)
        o_ref[:, h * hd:(h + 1) * hd] = o.astype(o_ref.dtype)


def xattn_core(q, kv, *, B, S, tm=512):
    T, XW = q.shape
    M = kv.shape[0] // B
    nt = S // tm
    return pl.pallas_call(
        _xattn_kernel,
        grid=(B, nt),
        in_specs=[pl.BlockSpec((tm, XW), lambda b, i: (b * nt + i, 0)),
                  pl.BlockSpec((M, XW), lambda b, i: (b, 0)),
                  pl.BlockSpec((M, XW), lambda b, i: (b, 1))],
        out_specs=pl.BlockSpec((tm, XW), lambda b, i: (b * nt + i, 0)),
        out_shape=jax.ShapeDtypeStruct((T, XW), BF16),
        compiler_params=_params("parallel", "parallel"),
        name="xattn_core",
    )(q, kv, kv)


def _softplus(x):
    return jnp.maximum(x, 0.0) + jnp.log1p(jnp.exp(-jnp.abs(x)))


def _sigmoid(x):
    return 1.0 / (1.0 + jnp.exp(-x))


def _gelu_tanh(g):
    return 0.5 * g * (1.0 + jnp.tanh(math.sqrt(2.0 / math.pi) * (g + 0.044715 * (g * g * g))))


def _blocked_scan(a_ref, x_ref, h_ref, gp_ref, gh_ref, ent_ref, l3_ref, d3_ref, e3_ref,
                  h0, reverse):
    tc = a_ref.shape[0]
    g = tc // SUBLANES
    n3 = g // SUBLANES
    order = tuple(range(SUBLANES - 1, -1, -1)) if reverse else tuple(range(SUBLANES))
    korder = tuple(range(n3 - 1, -1, -1)) if reverse else tuple(range(n3))
    last = order[-1]

    def strided(ref, r, n):
        return ref[pl.ds(r, n, stride=SUBLANES), :]

    loc, dec = {}, {}
    prev = None
    for r in order:
        a, x = strided(a_ref, r, g), strided(x_ref, r, g)
        loc[r] = x if prev is None else a * loc[prev] + x
        dec[r] = a if prev is None else a * dec[prev]
        prev = r
    gp_ref[...] = dec[last]
    gh_ref[...] = loc[last]

    loc2, dec2 = {}, {}
    prev = None
    for s in order:
        q, k = strided(gp_ref, s, n3), strided(gh_ref, s, n3)
        loc2[s] = k if prev is None else q * loc2[prev] + k
        dec2[s] = q if prev is None else q * dec2[prev]
        prev = s

    state = h0
    l3_ref[...] = loc2[last]
    d3_ref[...] = dec2[last]
    for k in korder:
        e3_ref[k:k + 1, :] = state
        state = l3_ref[k:k + 1, :] + d3_ref[k:k + 1, :] * state
    entering3 = e3_ref[...]

    entering = entering3
    for s in order:
        ent_ref[pl.ds(s, n3, stride=SUBLANES), :] = entering
        entering = loc2[s] + dec2[s] * entering3
    ent = ent_ref[...]
    for r in order:
        h_ref[pl.ds(r, g, stride=SUBLANES), :] = loc[r] + dec[r] * ent
    return state


def _lru_kernel(u_ref, up_ref, un_ref, gate_ref, cw_ref, cb_ref,
                waf_ref, baf_ref, wif_ref, bif_ref, lamf_ref,
                wab_ref, bab_ref, wib_ref, bib_ref, lamb_ref,
                y_ref, pad_ref, a_ref, x_ref, h_ref, gp_ref, gh_ref, ent_ref,
                l3_ref, d3_ref, e3_ref, hf_ref, uc_ref, st_ref):
    ph = pl.program_id(2)
    c = pl.program_id(3)
    nt = pl.num_programs(3)
    cc = jnp.where(ph == 0, c, nt - 1 - c)
    tc, bw = u_ref.shape
    h = SUBLANES
    row0 = pl.multiple_of(cc * tc, tc)
    lane_tiles = [slice(l * LANES, (l + 1) * LANES) for l in range(bw // LANES)]

    @pl.when(c == 0)
    def _():
        st_ref[...] = jnp.zeros_like(st_ref)

    def conv():
        pad_ref[0:h] = jnp.where(cc > 0, up_ref[...], 0.0)
        pad_ref[h:h + tc] = u_ref[...]
        pad_ref[h + tc:2 * h + tc] = jnp.where(cc < nt - 1, un_ref[...], 0.0)
        cw = cw_ref[...]
        uc = cb_ref[...]
        for t in range(CONV_WIDTH):
            off = h + t - CONV_LEFT
            uc = uc + pad_ref[off:off + tc] * cw[t:t + 1]
        return uc

    def sweep(uc, wa, ba, wi, bi, lam, reverse):
        ub = uc.astype(BF16)

        def gate(w_ref, b_ref):
            z = jnp.dot(ub, w_ref[...].astype(BF16), preferred_element_type=F32) + b_ref[...]
            return _sigmoid(z)

        r = gate(wa, ba)
        ig = gate(wi, bi)
        log_a = (-LRU_C) * r * _softplus(-lam[...])
        a = jnp.exp(log_a)
        x = jnp.sqrt(jnp.tanh(-log_a) * (1.0 + a * a)) * (ig * uc)
        for l, lanes in enumerate(lane_tiles):
            a_ref[l] = a[:, lanes]
            x_ref[l] = x[:, lanes]
        for l, lanes in enumerate(lane_tiles):
            st_ref[0:1, lanes] = _blocked_scan(
                a_ref.at[l], x_ref.at[l], h_ref.at[l], gp_ref.at[l], gh_ref.at[l], ent_ref.at[l],
                l3_ref.at[l], d3_ref.at[l], e3_ref.at[l], st_ref[0:1, lanes], reverse)

    @pl.when(ph == 0)
    def _():
        uc = conv()
        uc_ref[pl.ds(row0, tc), :] = uc
        sweep(uc, waf_ref, baf_ref, wif_ref, bif_ref, lamf_ref, False)
        for l in range(len(lane_tiles)):
            hf_ref[l, pl.ds(row0, tc), :] = h_ref[l]

    @pl.when(ph == 1)
    def _():
        sweep(uc_ref[pl.ds(row0, tc), :], wab_ref, bab_ref, wib_ref, bib_ref, lamb_ref, True)
        for l, lanes in enumerate(lane_tiles):
            hsum = hf_ref[l, pl.ds(row0, tc), :] + h_ref[l]
            y_ref[:, lanes] = (hsum * _gelu_tanh(gate_ref[:, lanes])).astype(y_ref.dtype)


def rglru(ug, layer, conv_w, conv_b, dirs, *, B, S, W, tc=1024):
    T = B * S
    bw = LRU_BLOCK_W
    nb = W // bw
    tc = min(tc, S)
    nt = S // tc
    h = SUBLANES
    assert tc % (h * h) == 0 and S % tc == 0
    last8 = T // h - 1

    def early(p, c):
        return c + p * (nt - 1 - c)

    def late(p, c):
        return nt - 1 - p * c

    in_specs = [
        pl.BlockSpec((tc, bw), lambda n, b, p, c: (b * nt + early(p, c), n)),
        pl.BlockSpec((h, bw), lambda n, b, p, c: (
            jnp.maximum((b * S + early(p, c) * tc) // h - 1, 0), n)),
        pl.BlockSpec((h, bw), lambda n, b, p, c: (
            jnp.minimum((b * S + (early(p, c) + 1) * tc) // h, last8), n)),
        pl.BlockSpec((tc, bw), lambda n, b, p, c: (b * nt + late(p, c), nb + n)),
        pl.BlockSpec((None, CONV_WIDTH, bw), lambda n, b, p, c: (layer, 0, n)),
        pl.BlockSpec((None, 1, bw), lambda n, b, p, c: (layer, 0, n))]
    args = [ug, ug, ug, ug, conv_w, conv_b.reshape(conv_b.shape[0], 1, W)]
    wspec = pl.BlockSpec((None, None, bw, bw), lambda n, b, p, c: (layer, n, 0, 0))
    vspec = pl.BlockSpec((None, None, 1, bw), lambda n, b, p, c: (layer, n, 0, 0))
    for wa, ba, wi, bi, lam in dirs:
        L = wa.shape[0]
        in_specs += [wspec, vspec, wspec, vspec, vspec]
        args += [wa, ba.reshape(L, nb, 1, bw), wi, bi.reshape(L, nb, 1, bw),
                 lam.reshape(L, nb, 1, bw)]
    g = tc // h
    nl = bw // LANES

    def tiles(rows):
        return pltpu.VMEM((nl, rows, LANES), F32)

    return pl.pallas_call(
        _lru_kernel,
        grid=(nb, B, 2, nt),
        in_specs=in_specs,
        out_specs=pl.BlockSpec((tc, bw), lambda n, b, p, c: (b * nt + late(p, c), n)),
        out_shape=jax.ShapeDtypeStruct((T, W), BF16),
        scratch_shapes=[pltpu.VMEM((tc + 2 * h, bw), F32),
                        tiles(tc),
                        tiles(tc),
                        tiles(tc),
                        tiles(g),
                        tiles(g),
                        tiles(g),
                        tiles(g // h),
                        tiles(g // h),
                        tiles(g // h),
                        tiles(S),
                        pltpu.VMEM((S, bw), F32),
                        pltpu.VMEM((h, bw), F32)],
        compiler_params=_params("parallel", "parallel", "arbitrary", "arbitrary"),
        name="rglru",
    )(*args)


def _lambda_init(layer_idx):
    return 0.8 - 0.6 * math.exp(-0.3 * layer_idx)


def kernel(x, mem, positions, attn_norm_g, attn_w_qkv, attn_lambda_q1, attn_lambda_k1, attn_lambda_q2, attn_lambda_k2, attn_subln_g, attn_w_o, rnn_norm_g, rnn_w_in, rnn_conv_w, rnn_conv_b, rnn_wa_f, rnn_ba_f, rnn_wi_f, rnn_bi_f, rnn_lam_f, rnn_wa_b, rnn_ba_b, rnn_wi_b, rnn_bi_b, rnn_lam_b, rnn_w_out, xattn_norm_g, xattn_mem_g, xattn_w_q, xattn_w_kv, xattn_w_o, mlp_norm_g, mlp_w1, mlp_w2, final_g):
    B, S, D = x.shape
    T = B * S
    depth = xattn_norm_g.shape[0]
    M = mem.shape[1]
    W = rnn_lam_f.shape[-1]

    h = x.reshape(T, D)
    mem2 = mem.reshape(B * M, D)
    rope = rope_tables(positions)
    lam4 = jnp.stack([attn_lambda_q1, attn_lambda_k1, attn_lambda_q2, attn_lambda_k2], axis=1)

    stats = None

    def normed(w3, layer, g2, **kw):
        if stats is None:
            return matmul(rmsnorm(h, g2, layer, BF16), w3, layer, **kw)
        return matmul(stats[0], w3, layer, norm=(g2[layer], stats[1]), **kw)

    nslab = mlp_w2.shape[1] // min(MM_TK, mlp_w2.shape[1])
    for i in range(depth):
        j = i // N_MIXERS
        if i % N_MIXERS == 0:
            qkv = normed(attn_w_qkv, j, attn_norm_g, out_dtype=BF16,
                         scale=DA_HEAD_DIM ** -0.5, rope=rope, rope_width=D)
            o = diff_attention(qkv, lam4, attn_subln_g, j, B=B, S=S, D=D,
                               lam_init=_lambda_init(i))
            h, *stats = matmul(o, attn_w_o, j, out_dtype=F32, res=h, stats_out=True)
        else:
            ug = normed(rnn_w_in, j, rnn_norm_g, out_dtype=F32)
            y = rglru(ug, j, rnn_conv_w, rnn_conv_b,
                      ((rnn_wa_f, rnn_ba_f, rnn_wi_f, rnn_bi_f, rnn_lam_f),
                       (rnn_wa_b, rnn_ba_b, rnn_wi_b, rnn_bi_b, rnn_lam_b)),
                      B=B, S=S, W=W)
            h, *stats = matmul(y, rnn_w_out, j, out_dtype=F32, res=h, stats_out=True)

        q = normed(xattn_w_q, i, xattn_norm_g, out_dtype=BF16, scale=X_HEAD_DIM ** -0.5)
        kv = matmul(rmsnorm(mem2, xattn_mem_g, i, BF16), xattn_w_kv, i, out_dtype=BF16)
        o = xattn_core(q, kv, B=B, S=S)
        h, *stats = matmul(o, xattn_w_o, i, out_dtype=F32, res=h, tm=512, tn=2048,
                           stats_out=True)

        hid = normed(mlp_w1, i, mlp_norm_g, out_dtype=BF16, relu2=True)
        for ks in range(nslab):
            if ks == nslab - 1 and i < depth - 1:
                h, *stats = matmul(hid, mlp_w2, i, out_dtype=F32, res=h, kslab=ks, stats_out=True)
            else:
                h = matmul(hid, mlp_w2, i, out_dtype=F32, res=h, kslab=ks)

    return rmsnorm(h, final_g.reshape(1, D), 0, F32).reshape(B, S, D)
```

```python
import functools
import math

import jax
import jax.numpy as jnp
from jax import lax
from jax.experimental import pallas as pl
from jax.experimental.pallas import tpu as pltpu

F32 = jnp.float32
BF16 = jnp.bfloat16

EPS = 1e-6
ROPE_THETA = 500000.0
DA_HEAD_DIM = 128
ROT_DIM = DA_HEAD_DIM // 4
LRU_BLOCK_W = 256
LRU_C = 8.0
CONV_WIDTH = 4
CONV_LEFT = 2
X_HEADS = 4
X_HEAD_DIM = 128
N_MIXERS = 2

LANES = 128
SUBLANES = 8
VMEM_LIMIT_BYTES = 56 * 1024 * 1024
MM_TK = 4096
EXP_PANEL = 64


def _params(*semantics):
    return pltpu.CompilerParams(dimension_semantics=semantics,
                                vmem_limit_bytes=VMEM_LIMIT_BYTES)


def _rope_table_kernel(pos_ref, invf_ref, cos_ref, sa_ref, sb_ref):
    ang = pos_ref[...].astype(F32) * invf_ref[...]
    lane = lax.broadcasted_iota(jnp.int32, ang.shape, 1)
    half = ROT_DIM // 2
    c, s = jnp.cos(ang), jnp.sin(ang)
    cos_ref[...] = jnp.where(lane < ROT_DIM, c, 1.0)
    sa_ref[...] = jnp.where(lane < half, -s, 0.0)
    sb_ref[...] = jnp.where((lane >= half) & (lane < ROT_DIM), s, 0.0)


def rope_tables(positions, tm=1024):
    T = positions.size
    tm = min(tm, T)
    half = ROT_DIM // 2
    inv_freq = ROPE_THETA ** (-jnp.arange(0, ROT_DIM, 2, dtype=F32) / ROT_DIM)
    invf = jnp.tile(inv_freq, LANES // half)[None, :]
    tab = jax.ShapeDtypeStruct((T, LANES), F32)
    spec = pl.BlockSpec((tm, LANES), lambda i: (i, 0))
    return pl.pallas_call(
        _rope_table_kernel,
        grid=(T // tm,),
        in_specs=[pl.BlockSpec((tm, 1), lambda i: (i, 0)),
                  pl.BlockSpec((1, LANES), lambda i: (0, 0))],
        out_specs=[spec, spec, spec],
        out_shape=[tab, tab, tab],
        compiler_params=_params("parallel"),
        name="rope_tables",
    )(positions.reshape(T, 1), invf)


def _rmsnorm_kernel(x_ref, g_ref, o_ref):
    x = x_ref[...].astype(F32)
    ms = jnp.mean(x * x, axis=-1, keepdims=True)
    o_ref[...] = (x * lax.rsqrt(ms + EPS) * g_ref[...]).astype(o_ref.dtype)


def rmsnorm(x, g2, layer, out_dtype, tm=256):
    T, D = x.shape
    tm = min(tm, T)
    g3 = g2.reshape(g2.shape[0], 1, D)
    return pl.pallas_call(
        _rmsnorm_kernel,
        grid=(T // tm,),
        in_specs=[pl.BlockSpec((tm, D), lambda i: (i, 0)),
                  pl.BlockSpec((None, 1, D), lambda i: (layer, 0, 0))],
        out_specs=pl.BlockSpec((tm, D), lambda i: (i, 0)),
        out_shape=jax.ShapeDtypeStruct((T, D), out_dtype),
        compiler_params=_params("parallel"),
        name="rmsnorm",
    )(x, g3)


def _rope_cols(t, cos, sa, sb):
    half = ROT_DIM // 2
    return (t * cos + pltpu.roll(t, LANES - half, 1) * sa + pltpu.roll(t, half, 1) * sb)


def _mm_kernel(*refs, relu2, scale, has_res, rope_tiles, norm_dim, stats_out):
    it = iter(refs)
    a_ref, w_ref = next(it), next(it)
    res_ref = next(it) if has_res else None
    if rope_tiles:
        cos_ref, sa_ref, sb_ref = next(it), next(it), next(it)
    if norm_dim:
        g_ref, ssq_ref = next(it), next(it)
    o_ref = next(it)
    if stats_out:
        ob_ref, osq_ref = next(it), next(it)
    wbf_ref = next(it)
    if stats_out:
        acc_sq_ref = next(it)
    j, i = pl.program_id(0), pl.program_id(1)

    if stats_out:
        @pl.when(j == 0)
        def _():
            acc_sq_ref[i] = jnp.zeros(acc_sq_ref.shape[1:], F32)

    def tile(wb):
        acc = jnp.dot(a_ref[...], wb, preferred_element_type=F32)
        if norm_dim:
            ssq = jnp.sum(ssq_ref[...], axis=1, keepdims=True)
            acc = acc * lax.rsqrt(ssq * (1.0 / norm_dim) + EPS)

        if rope_tiles:
            rotary = j < 2 * rope_tiles
            qs = jnp.where(j < rope_tiles, scale, 1.0).astype(F32)
            cos = jnp.where(rotary, cos_ref[...], 1.0) * qs
            sa = jnp.where(rotary, sa_ref[...], 0.0) * qs
            sb = jnp.where(rotary, sb_ref[...], 0.0) * qs
            for c in range(acc.shape[1] // LANES):
                t = _rope_cols(acc[:, c * LANES:(c + 1) * LANES], cos, sa, sb)
                o_ref[:, c * LANES:(c + 1) * LANES] = t.astype(o_ref.dtype)
            return
        if relu2:
            r = jnp.maximum(acc, 0.0)
            acc = r * r
        if scale is not None:
            acc = acc * scale
        if has_res:
            acc = acc + res_ref[...]
        o_ref[...] = acc.astype(o_ref.dtype)
        if stats_out:
            ob_ref[...] = acc.astype(BF16)
            sq = acc * acc
            part = sq[:, 0:LANES]
            for c in range(1, acc.shape[1] // LANES):
                part = part + sq[:, c * LANES:(c + 1) * LANES]
            total = acc_sq_ref[i] + part
            acc_sq_ref[i] = total
            osq_ref[...] = total

    @pl.when(i == 0)
    def _():
        w = w_ref[...]
        if norm_dim:
            w = w * g_ref[...]
        wb = w.astype(BF16)
        wbf_ref[...] = wb
        tile(wb)

    @pl.when(i > 0)
    def _():
        tile(wbf_ref[...])


def matmul(a, w3, layer, *, out_dtype, kslab=0, tm=1024, tn=512, tk=MM_TK, relu2=False,
           scale=None, res=None, rope=None, rope_width=None, norm=None, stats_out=False):
    M = a.shape[0]
    _, K, N = w3.shape
    tm, tn, tk = min(tm, M), min(tn, N), min(tk, K)
    nj, ni = N // tn, M // tm
    in_specs = [pl.BlockSpec((tm, tk), lambda j, i: (i, kslab)),
                pl.BlockSpec((None, tk, tn), lambda j, i: (layer, kslab, j))]
    args = [a, w3]
    if res is not None:
        in_specs.append(pl.BlockSpec((tm, tn), lambda j, i: (i, j)))
        args.append(res)
    rope_tiles = 0
    if rope is not None:
        assert rope_width % tn == 0
        rope_tiles = rope_width // tn
        in_specs += [pl.BlockSpec((tm, LANES), lambda j, i: (i, 0))] * 3
        args += list(rope)
    if norm is not None:
        assert tk == K, "the folded rmsnorm needs the whole row in one contraction slab"
        g, ssq = norm
        in_specs += [pl.BlockSpec((tk, 1), lambda j, i: (0, 0)),
                     pl.BlockSpec((tm, LANES), lambda j, i: (i, 0))]
        args += [g.reshape(K, 1), ssq]
    out_specs = pl.BlockSpec((tm, tn), lambda j, i: (i, j))
    out_shape = jax.ShapeDtypeStruct((M, N), out_dtype)
    scratch = [pltpu.VMEM((tk, tn), BF16)]
    if stats_out:
        out_specs = [out_specs, pl.BlockSpec((tm, tn), lambda j, i: (i, j)),
                     pl.BlockSpec((tm, LANES), lambda j, i: (jnp.where(j == nj - 1, i, 0), 0))]
        out_shape = [out_shape, jax.ShapeDtypeStruct((M, N), BF16),
                     jax.ShapeDtypeStruct((M, LANES), F32)]
        scratch.append(pltpu.VMEM((ni, tm, LANES), F32))
    kern = functools.partial(_mm_kernel, relu2=relu2, scale=scale, has_res=res is not None,
                             rope_tiles=rope_tiles, norm_dim=K if norm is not None else 0,
                             stats_out=stats_out)
    return pl.pallas_call(
        kern,
        grid=(nj, ni),
        in_specs=in_specs,
        out_specs=out_specs,
        out_shape=out_shape,
        scratch_shapes=scratch,
        compiler_params=_params("arbitrary", "arbitrary"),
        name="matmul",
    )(*args)


def _dattn_kernel(lam_ref, q_ref, k_ref, v_ref, g_ref, o_ref, *, kc, rb, lam_init):
    tq = q_ref.shape[0]
    nkc = k_ref.shape[0] // kc
    nrb = tq // rb
    npan = rb // EXP_PANEL
    d = DA_HEAD_DIM
    lv = lam_ref[...]
    lam = (jnp.exp(jnp.sum(lv[0:1] * lv[1:2], axis=1, keepdims=True))
           - jnp.exp(jnp.sum(lv[2:3] * lv[3:4], axis=1, keepdims=True)) + lam_init)
    maps = (0, 1)

    def scores(r, c, mp, sv):
        for m in maps:
            kk = k_ref[c * kc:(c + 1) * kc, m * d:(m + 1) * d]
            s = lax.dot_general(q_ref[r * rb:(r + 1) * rb, m * d:(m + 1) * d], kk,
                                (((1,), (1,)), ((), ())), preferred_element_type=F32)
            sv[m, c] = s
            cm = s[:, 0:LANES]
            for t in range(1, kc // LANES):
                cm = jnp.maximum(cm, s[:, t * LANES:(t + 1) * LANES])
            mp[m] = cm if mp[m] is None else jnp.maximum(mp[m], cm)

    def exps(c, mb, lp, sv, pv):
        for m in maps:
            for i in range(npan):
                rows = slice(i * EXP_PANEL, (i + 1) * EXP_PANEL)
                cl = None
                for t in range(kc // LANES):
                    pt = jnp.exp(sv[m, c][rows, t * LANES:(t + 1) * LANES] - mb[m][i])
                    cl = pt if cl is None else cl + pt
                    pv[m, c, i, t] = pt.astype(BF16)
                lp[m][i] = cl if lp[m][i] is None else lp[m][i] + cl

    def weighted(r, lp, pv):
        hk = k_ref.shape[0] // 2
        heads = []
        for m in maps:
            p = jnp.concatenate(
                [jnp.concatenate(
                    [jnp.concatenate([pv[m, c, i, t] for t in range(kc // LANES)], axis=1)
                     for i in range(npan)], axis=0) for c in range(nkc)], axis=1)
            pvm = (jnp.dot(p[:, 0:hk], v_ref[0:hk, :], preferred_element_type=F32)
                   + jnp.dot(p[:, hk:], v_ref[hk:, :], preferred_element_type=F32))
            l = jnp.concatenate([jnp.sum(x, axis=1, keepdims=True) for x in lp[m]], axis=0)
            heads.append(pvm / l)
        o = heads[0] - lam * heads[1]
        ms = jnp.mean(o * o, axis=1, keepdims=True)
        o = o * lax.rsqrt(ms + EPS) * g_ref[...] * (1.0 - lam_init)
        o_ref[r * rb:(r + 1) * rb, :] = o.astype(o_ref.dtype)

    mp, mb, lp, sv, pv = {}, {}, {}, {}, {}
    for stage in range(nrb + 2):
        ra, re, rp = stage, stage - 1, stage - 2
        if 0 <= rp < nrb:
            weighted(rp, lp[rp], pv[rp])
        if 0 <= ra < nrb:
            mp[ra] = [None, None]
            sv[ra] = {}
        if 0 <= re < nrb:
            lp[re] = [[None] * npan for _ in maps]
            pv[re] = {}
        for c in range(nkc):
            if 0 <= re < nrb:
                exps(c, mb[re], lp[re], sv[re], pv[re])
            if 0 <= ra < nrb:
                scores(ra, c, mp[ra], sv[ra])
        if 0 <= ra < nrb:
            mb[ra] = []
            for m in maps:
                rowmax = jnp.max(mp[ra][m], axis=1, keepdims=True)
                mb[ra].append([jnp.broadcast_to(rowmax[i * EXP_PANEL:(i + 1) * EXP_PANEL],
                                                (EXP_PANEL, LANES)) for i in range(npan)])


def diff_attention(qkv, lam4, subln_g, layer, *, B, S, D, lam_init, tq=1024, rb=256, kc=512):
    T = B * S
    hw = 2 * DA_HEAD_DIM
    H = D // hw
    tq = min(tq, S)
    rb = min(rb, tq)
    nq = S // tq
    g3 = subln_g.reshape(subln_g.shape[0], 1, hw)
    kern = functools.partial(_dattn_kernel, kc=kc, rb=rb, lam_init=lam_init)
    return pl.pallas_call(
        kern,
        grid=(B, H, nq),
        in_specs=[pl.BlockSpec((None, 4, DA_HEAD_DIM), lambda b, h, i: (layer, 0, 0)),
                  pl.BlockSpec((tq, hw), lambda b, h, i: (b * nq + i, h)),
                  pl.BlockSpec((S, hw), lambda b, h, i: (b, H + h)),
                  pl.BlockSpec((S, hw), lambda b, h, i: (b, 2 * H + h)),
                  pl.BlockSpec((None, 1, hw), lambda b, h, i: (layer, 0, 0))],
        out_specs=pl.BlockSpec((tq, hw), lambda b, h, i: (b * nq + i, h)),
        out_shape=jax.ShapeDtypeStruct((T, D), BF16),
        compiler_params=_params("parallel", "parallel", "arbitrary"),
        name="diff_attention",
    )(lam4, qkv, qkv, qkv, g3)


def _xattn_kernel(q_ref, k_ref, v_ref, o_ref):
    hd = X_HEAD_DIM
    for h in range(X_HEADS):
        q = q_ref[:, h * hd:(h + 1) * hd]
        k = k_ref[:, h * hd:(h + 1) * hd]
        v = v_ref[:, h * hd:(h + 1) * hd]
        s = lax.dot_general(q, k, (((1,), (1,)), ((), ())), preferred_element_type=F32)
        p = jnp.exp(s - jnp.max(s, axis=1, keepdims=True))
        p = p / jnp.sum(p, axis=1, keepdims=True)
        o = jnp.dot(p.astype(BF16), v, preferred_element_type=F32)
        o_ref[:, h * hd:(h + 1) * hd] = o.astype(o_ref.dtype)


def xattn_core(q, kv, *, B, S, tm=512):
    T, XW = q.shape
    M = kv.shape[0] // B
    nt = S // tm
    return pl.pallas_call(
        _xattn_kernel,
        grid=(B, nt),
        in_specs=[pl.BlockSpec((tm, XW), lambda b, i: (b * nt + i, 0)),
                  pl.BlockSpec((M, XW), lambda b, i: (b, 0)),
                  pl.BlockSpec((M, XW), lambda b, i: (b, 1))],
        out_specs=pl.BlockSpec((tm, XW), lambda b, i: (b * nt + i, 0)),
        out_shape=jax.ShapeDtypeStruct((T, XW), BF16),
        compiler_params=_params("parallel", "parallel"),
        name="xattn_core",
    )(q, kv, kv)


def _softplus(x):
    return jnp.maximum(x, 0.0) + jnp.log1p(jnp.exp(-jnp.abs(x)))


def _sigmoid(x):
    return 1.0 / (1.0 + jnp.exp(-x))


def _gelu_tanh(g):
    return 0.5 * g * (1.0 + jnp.tanh(math.sqrt(2.0 / math.pi) * (g + 0.044715 * (g * g * g))))


def _blocked_scan(a_ref, x_ref, h_ref, gp_ref, gh_ref, ent_ref, l3_ref, d3_ref, e3_ref,
                  h0, reverse):
    tc = a_ref.shape[0]
    g = tc // SUBLANES
    n3 = g // SUBLANES
    order = tuple(range(SUBLANES - 1, -1, -1)) if reverse else tuple(range(SUBLANES))
    korder = tuple(range(n3 - 1, -1, -1)) if reverse else tuple(range(n3))
    last = order[-1]

    def strided(ref, r, n):
        return ref[pl.ds(r, n, stride=SUBLANES), :]

    loc, dec = {}, {}
    prev = None
    for r in order:
        a, x = strided(a_ref, r, g), strided(x_ref, r, g)
        loc[r] = x if prev is None else a * loc[prev] + x
        dec[r] = a if prev is None else a * dec[prev]
        prev = r
    gp_ref[...] = dec[last]
    gh_ref[...] = loc[last]

    loc2, dec2 = {}, {}
    prev = None
    for s in order:
        q, k = strided(gp_ref, s, n3), strided(gh_ref, s, n3)
        loc2[s] = k if prev is None else q * loc2[prev] + k
        dec2[s] = q if prev is None else q * dec2[prev]
        prev = s

    state = h0
    l3_ref[...] = loc2[last]
    d3_ref[...] = dec2[last]
    for k in korder:
        e3_ref[k:k + 1, :] = state
        state = l3_ref[k:k + 1, :] + d3_ref[k:k + 1, :] * state
    entering3 = e3_ref[...]

    entering = entering3
    for s in order:
        ent_ref[pl.ds(s, n3, stride=SUBLANES), :] = entering
        entering = loc2[s] + dec2[s] * entering3
    ent = ent_ref[...]
    for r in order:
        h_ref[pl.ds(r, g, stride=SUBLANES), :] = loc[r] + dec[r] * ent
    return state


def _lru_kernel(u_ref, up_ref, un_ref, gate_ref, cw_ref, cb_ref,
                waf_ref, baf_ref, wif_ref, bif_ref, lamf_ref,
                wab_ref, bab_ref, wib_ref, bib_ref, lamb_ref,
                y_ref, pad_ref, a_ref, x_ref, h_ref, gp_ref, gh_ref, ent_ref,
                l3_ref, d3_ref, e3_ref, hf_ref, uc_ref, st_ref):
    ph = pl.program_id(2)
    c = pl.program_id(3)
    nt = pl.num_programs(3)
    cc = jnp.where(ph == 0, c, nt - 1 - c)
    tc, bw = u_ref.shape
    h = SUBLANES
    row0 = pl.multiple_of(cc * tc, tc)
    lane_tiles = [slice(l * LANES, (l + 1) * LANES) for l in range(bw // LANES)]

    @pl.when(c == 0)
    def _():
        st_ref[...] = jnp.zeros_like(st_ref)

    def conv():
        pad_ref[0:h] = jnp.where(cc > 0, up_ref[...], 0.0)
        pad_ref[h:h + tc] = u_ref[...]
        pad_ref[h + tc:2 * h + tc] = jnp.where(cc < nt - 1, un_ref[...], 0.0)
        cw = cw_ref[...]
        uc = cb_ref[...]
        for t in range(CONV_WIDTH):
            off = h + t - CONV_LEFT
            uc = uc + pad_ref[off:off + tc] * cw[t:t + 1]
        return uc

    def sweep(uc, wa, ba, wi, bi, lam, reverse):
        ub = uc.astype(BF16)

        def gate(w_ref, b_ref):
            z = jnp.dot(ub, w_ref[...].astype(BF16), preferred_element_type=F32) + b_ref[...]
            return _sigmoid(z)

        r = gate(wa, ba)
        ig = gate(wi, bi)
        log_a = (-LRU_C) * r * _softplus(-lam[...])
        a = jnp.exp(log_a)
        x = jnp.sqrt(jnp.tanh(-log_a) * (1.0 + a * a)) * (ig * uc)
        for l, lanes in enumerate(lane_tiles):
            a_ref[l] = a[:, lanes]
            x_ref[l] = x[:, lanes]
        for l, lanes in enumerate(lane_tiles):
            st_ref[0:1, lanes] = _blocked_scan(
                a_ref.at[l], x_ref.at[l], h_ref.at[l], gp_ref.at[l], gh_ref.at[l], ent_ref.at[l],
                l3_ref.at[l], d3_ref.at[l], e3_ref.at[l], st_ref[0:1, lanes], reverse)

    @pl.when(ph == 0)
    def _():
        uc = conv()
        uc_ref[pl.ds(row0, tc), :] = uc
        sweep(uc, waf_ref, baf_ref, wif_ref, bif_ref, lamf_ref, False)
        for l in range(len(lane_tiles)):
            hf_ref[l, pl.ds(row0, tc), :] = h_ref[l]

    @pl.when(ph == 1)
    def _():
        sweep(uc_ref[pl.ds(row0, tc), :], wab_ref, bab_ref, wib_ref, bib_ref, lamb_ref, True)
        for l, lanes in enumerate(lane_tiles):
            hsum = hf_ref[l, pl.ds(row0, tc), :] + h_ref[l]
            y_ref[:, lanes] = (hsum * _gelu_tanh(gate_ref[:, lanes])).astype(y_ref.dtype)


def rglru(ug, layer, conv_w, conv_b, dirs, *, B, S, W, tc=1024):
    T = B * S
    bw = LRU_BLOCK_W
    nb = W // bw
    tc = min(tc, S)
    nt = S // tc
    h = SUBLANES
    assert tc % (h * h) == 0 and S % tc == 0
    last8 = T // h - 1

    def early(p, c):
        return c + p * (nt - 1 - c)

    def late(p, c):
        return nt - 1 - p * c

    in_specs = [
        pl.BlockSpec((tc, bw), lambda n, b, p, c: (b * nt + early(p, c), n)),
        pl.BlockSpec((h, bw), lambda n, b, p, c: (
            jnp.maximum((b * S + early(p, c) * tc) // h - 1, 0), n)),
        pl.BlockSpec((h, bw), lambda n, b, p, c: (
            jnp.minimum((b * S + (early(p, c) + 1) * tc) // h, last8), n)),
        pl.BlockSpec((tc, bw), lambda n, b, p, c: (b * nt + late(p, c), nb + n)),
        pl.BlockSpec((None, CONV_WIDTH, bw), lambda n, b, p, c: (layer, 0, n)),
        pl.BlockSpec((None, 1, bw), lambda n, b, p, c: (layer, 0, n))]
    args = [ug, ug, ug, ug, conv_w, conv_b.reshape(conv_b.shape[0], 1, W)]
    wspec = pl.BlockSpec((None, None, bw, bw), lambda n, b, p, c: (layer, n, 0, 0))
    vspec = pl.BlockSpec((None, None, 1, bw), lambda n, b, p, c: (layer, n, 0, 0))
    for wa, ba, wi, bi, lam in dirs:
        L = wa.shape[0]
        in_specs += [wspec, vspec, wspec, vspec, vspec]
        args += [wa, ba.reshape(L, nb, 1, bw), wi, bi.reshape(L, nb, 1, bw),
                 lam.reshape(L, nb, 1, bw)]
    g = tc // h
    nl = bw // LANES

    def tiles(rows):
        return pltpu.VMEM((nl, rows, LANES), F32)

    return pl.pallas_call(
        _lru_kernel,
        grid=(nb, B, 2, nt),
        in_specs=in_specs,
        out_specs=pl.BlockSpec((tc, bw), lambda n, b, p, c: (b * nt + late(p, c), n)),
        out_shape=jax.ShapeDtypeStruct((T, W), BF16),
        scratch_shapes=[pltpu.VMEM((tc + 2 * h, bw), F32),
                        tiles(tc),
                        tiles(tc),
                        tiles(tc),
                        tiles(g),
                        tiles(g),
                        tiles(g),
                        tiles(g // h),
                        tiles(g // h),
                        tiles(g // h),
                        tiles(S),
                        pltpu.VMEM((S, bw), F32),
                        pltpu.VMEM((h, bw), F32)],
        compiler_params=_params("parallel", "parallel", "arbitrary", "arbitrary"),
        name="rglru",
    )(*args)


def _lambda_init(layer_idx):
    return 0.8 - 0.6 * math.exp(-0.3 * layer_idx)


def kernel(x, mem, positions, attn_norm_g, attn_w_qkv, attn_lambda_q1, attn_lambda_k1, attn_lambda_q2, attn_lambda_k2, attn_subln_g, attn_w_o, rnn_norm_g, rnn_w_in, rnn_conv_w, rnn_conv_b, rnn_wa_f, rnn_ba_f, rnn_wi_f, rnn_bi_f, rnn_lam_f, rnn_wa_b, rnn_ba_b, rnn_wi_b, rnn_bi_b, rnn_lam_b, rnn_w_out, xattn_norm_g, xattn_mem_g, xattn_w_q, xattn_w_kv, xattn_w_o, mlp_norm_g, mlp_w1, mlp_w2, final_g):
    B, S, D = x.shape
    T = B * S
    depth = xattn_norm_g.shape[0]
    M = mem.shape[1]
    W = rnn_lam_f.shape[-1]

    h = x.reshape(T, D)
    mem2 = mem.reshape(B * M, D)
    rope = rope_tables(positions)
    lam4 = jnp.stack([attn_lambda_q1, attn_lambda_k1, attn_lambda_q2, attn_lambda_k2], axis=1)

    stats = None

    def normed(w3, layer, g2, **kw):
        if stats is None:
            return matmul(rmsnorm(h, g2, layer, BF16), w3, layer, **kw)
        return matmul(stats[0], w3, layer, norm=(g2[layer], stats[1]), **kw)

    nslab = mlp_w2.shape[1] // min(MM_TK, mlp_w2.shape[1])
    for i in range(depth):
        j = i // N_MIXERS
        if i % N_MIXERS == 0:
            qkv = normed(attn_w_qkv, j, attn_norm_g, out_dtype=BF16,
                         scale=DA_HEAD_DIM ** -0.5, rope=rope, rope_width=D)
            o = diff_attention(qkv, lam4, attn_subln_g, j, B=B, S=S, D=D,
                               lam_init=_lambda_init(i))
            h, *stats = matmul(o, attn_w_o, j, out_dtype=F32, res=h, stats_out=True)
        else:
            ug = normed(rnn_w_in, j, rnn_norm_g, out_dtype=F32)
            y = rglru(ug, j, rnn_conv_w, rnn_conv_b,
                      ((rnn_wa_f, rnn_ba_f, rnn_wi_f, rnn_bi_f, rnn_lam_f),
                       (rnn_wa_b, rnn_ba_b, rnn_wi_b, rnn_bi_b, rnn_lam_b)),
                      B=B, S=S, W=W)
            h, *stats = matmul(y, rnn_w_out, j, out_dtype=F32, res=h, stats_out=True)

        q = normed(xattn_w_q, i, xattn_norm_g, out_dtype=BF16, scale=X_HEAD_DIM ** -0.5)
        kv = matmul(rmsnorm(mem2, xattn_mem_g, i, BF16), xattn_w_kv, i, out_dtype=BF16)
        o = xattn_core(q, kv, B=B, S=S)
        h, *stats = matmul(o, xattn_w_o, i, out_dtype=F32, res=h, tm=512, tn=2048,
                           stats_out=True)

        hid = normed(mlp_w1, i, mlp_norm_g, out_dtype=BF16, relu2=True)
        for ks in range(nslab):
            if ks == nslab - 1 and i < depth - 1:
                h, *stats = matmul(hid, mlp_w2, i, out_dtype=F32, res=h, kslab=ks, stats_out=True)
            else:
                h = matmul(hid, mlp_w2, i, out_dtype=F32, res=h, kslab=ks)

    return rmsnorm(h, final_g.reshape(1, D), 0, F32).reshape(B, S, D)
```

```python
import functools
import math

import jax
import jax.numpy as jnp
from jax import lax
from jax.experimental import pallas as pl
from jax.experimental.pallas import tpu as pltpu

F32 = jnp.float32
BF16 = jnp.bfloat16

EPS = 1e-6
ROPE_THETA = 500000.0
DA_HEAD_DIM = 128
ROT_DIM = DA_HEAD_DIM // 4
LRU_BLOCK_W = 256
LRU_C = 8.0
CONV_WIDTH = 4
CONV_LEFT = 2
X_HEADS = 4
X_HEAD_DIM = 128
N_MIXERS = 2

LANES = 128
SUBLANES = 8
VMEM_LIMIT_BYTES = 56 * 1024 * 1024
MM_TK = 4096
EXP_PANEL = 64


def _params(*semantics):
    return pltpu.CompilerParams(dimension_semantics=semantics,
                                vmem_limit_bytes=VMEM_LIMIT_BYTES)


def _rope_table_kernel(pos_ref, invf_ref, cos_ref, sa_ref, sb_ref):
    ang = pos_ref[...].astype(F32) * invf_ref[...]
    lane = lax.broadcasted_iota(jnp.int32, ang.shape, 1)
    half = ROT_DIM // 2
    c, s = jnp.cos(ang), jnp.sin(ang)
    cos_ref[...] = jnp.where(lane < ROT_DIM, c, 1.0)
    sa_ref[...] = jnp.where(lane < half, -s, 0.0)
    sb_ref[...] = jnp.where((lane >= half) & (lane < ROT_DIM), s, 0.0)


def rope_tables(positions, tm=1024):
    T = positions.size
    tm = min(tm, T)
    half = ROT_DIM // 2
    inv_freq = ROPE_THETA ** (-jnp.arange(0, ROT_DIM, 2, dtype=F32) / ROT_DIM)
    invf = jnp.tile(inv_freq, LANES // half)[None, :]
    tab = jax.ShapeDtypeStruct((T, LANES), F32)
    spec = pl.BlockSpec((tm, LANES), lambda i: (i, 0))
    return pl.pallas_call(
        _rope_table_kernel,
        grid=(T // tm,),
        in_specs=[pl.BlockSpec((tm, 1), lambda i: (i, 0)),
                  pl.BlockSpec((1, LANES), lambda i: (0, 0))],
        out_specs=[spec, spec, spec],
        out_shape=[tab, tab, tab],
        compiler_params=_params("parallel"),
        name="rope_tables",
    )(positions.reshape(T, 1), invf)


def _rmsnorm_kernel(x_ref, g_ref, o_ref):
    x = x_ref[...].astype(F32)
    ms = jnp.mean(x * x, axis=-1, keepdims=True)
    o_ref[...] = (x * lax.rsqrt(ms + EPS) * g_ref[...]).astype(o_ref.dtype)


def rmsnorm(x, g2, layer, out_dtype, tm=256):
    T, D = x.shape
    tm = min(tm, T)
    g3 = g2.reshape(g2.shape[0], 1, D)
    return pl.pallas_call(
        _rmsnorm_kernel,
        grid=(T // tm,),
        in_specs=[pl.BlockSpec((tm, D), lambda i: (i, 0)),
                  pl.BlockSpec((None, 1, D), lambda i: (layer, 0, 0))],
        out_specs=pl.BlockSpec((tm, D), lambda i: (i, 0)),
        out_shape=jax.ShapeDtypeStruct((T, D), out_dtype),
        compiler_params=_params("parallel"),
        name="rmsnorm",
    )(x, g3)


def _rope_cols(t, cos, sa, sb):
    half = ROT_DIM // 2
    return (t * cos + pltpu.roll(t, LANES - half, 1) * sa + pltpu.roll(t, half, 1) * sb)


def _mm_kernel(*refs, relu2, scale, has_res, rope_tiles, norm_dim, stats_out):
    it = iter(refs)
    a_ref, w_ref = next(it), next(it)
    res_ref = next(it) if has_res else None
    if rope_tiles:
        cos_ref, sa_ref, sb_ref = next(it), next(it), next(it)
    if norm_dim:
        g_ref, ssq_ref = next(it), next(it)
    o_ref = next(it)
    if stats_out:
        ob_ref, osq_ref = next(it), next(it)
    wbf_ref = next(it)
    if stats_out:
        acc_sq_ref = next(it)
    j, i = pl.program_id(0), pl.program_id(1)

    if stats_out:
        @pl.when(j == 0)
        def _():
            acc_sq_ref[i] = jnp.zeros(acc_sq_ref.shape[1:], F32)

    def tile(wb):
        acc = jnp.dot(a_ref[...], wb, preferred_element_type=F32)
        if norm_dim:
            ssq = jnp.sum(ssq_ref[...], axis=1, keepdims=True)
            acc = acc * lax.rsqrt(ssq * (1.0 / norm_dim) + EPS)

        if rope_tiles:
            rotary = j < 2 * rope_tiles
            qs = jnp.where(j < rope_tiles, scale, 1.0).astype(F32)
            cos = jnp.where(rotary, cos_ref[...], 1.0) * qs
            sa = jnp.where(rotary, sa_ref[...], 0.0) * qs
            sb = jnp.where(rotary, sb_ref[...], 0.0) * qs
            for c in range(acc.shape[1] // LANES):
                t = _rope_cols(acc[:, c * LANES:(c + 1) * LANES], cos, sa, sb)
                o_ref[:, c * LANES:(c + 1) * LANES] = t.astype(o_ref.dtype)
            return
        if relu2:
            r = jnp.maximum(acc, 0.0)
            acc = r * r
        if scale is not None:
            acc = acc * scale
        if has_res:
            acc = acc + res_ref[...]
        o_ref[...] = acc.astype(o_ref.dtype)
        if stats_out:
            ob_ref[...] = acc.astype(BF16)
            sq = acc * acc
            part = sq[:, 0:LANES]
            for c in range(1, acc.shape[1] // LANES):
                part = part + sq[:, c * LANES:(c + 1) * LANES]
            total = acc_sq_ref[i] + part
            acc_sq_ref[i] = total
            osq_ref[...] = total

    @pl.when(i == 0)
    def _():
        w = w_ref[...]
        if norm_dim:
            w = w * g_ref[...]
        wb = w.astype(BF16)
        wbf_ref[...] = wb
        tile(wb)

    @pl.when(i > 0)
    def _():
        tile(wbf_ref[...])


def matmul(a, w3, layer, *, out_dtype, kslab=0, tm=1024, tn=512, tk=MM_TK, relu2=False,
           scale=None, res=None, rope=None, rope_width=None, norm=None, stats_out=False):
    M = a.shape[0]
    _, K, N = w3.shape
    tm, tn, tk = min(tm, M), min(tn, N), min(tk, K)
    nj, ni = N // tn, M // tm
    in_specs = [pl.BlockSpec((tm, tk), lambda j, i: (i, kslab)),
                pl.BlockSpec((None, tk, tn), lambda j, i: (layer, kslab, j))]
    args = [a, w3]
    if res is not None:
        in_specs.append(pl.BlockSpec((tm, tn), lambda j, i: (i, j)))
        args.append(res)
    rope_tiles = 0
    if rope is not None:
        assert rope_width % tn == 0
        rope_tiles = rope_width // tn
        in_specs += [pl.BlockSpec((tm, LANES), lambda j, i: (i, 0))] * 3
        args += list(rope)
    if norm is not None:
        assert tk == K, "the folded rmsnorm needs the whole row in one contraction slab"
        g, ssq = norm
        in_specs += [pl.BlockSpec((tk, 1), lambda j, i: (0, 0)),
                     pl.BlockSpec((tm, LANES), lambda j, i: (i, 0))]
        args += [g.reshape(K, 1), ssq]
    out_specs = pl.BlockSpec((tm, tn), lambda j, i: (i, j))
    out_shape = jax.ShapeDtypeStruct((M, N), out_dtype)
    scratch = [pltpu.VMEM((tk, tn), BF16)]
    if stats_out:
        out_specs = [out_specs, pl.BlockSpec((tm, tn), lambda j, i: (i, j)),
                     pl.BlockSpec((tm, LANES), lambda j, i: (jnp.where(j == nj - 1, i, 0), 0))]
        out_shape = [out_shape, jax.ShapeDtypeStruct((M, N), BF16),
                     jax.ShapeDtypeStruct((M, LANES), F32)]
        scratch.append(pltpu.VMEM((ni, tm, LANES), F32))
    kern = functools.partial(_mm_kernel, relu2=relu2, scale=scale, has_res=res is not None,
                             rope_tiles=rope_tiles, norm_dim=K if norm is not None else 0,
                             stats_out=stats_out)
    return pl.pallas_call(
        kern,
        grid=(nj, ni),
        in_specs=in_specs,
        out_specs=out_specs,
        out_shape=out_shape,
        scratch_shapes=scratch,
        compiler_params=_params("arbitrary", "arbitrary"),
        name="matmul",
    )(*args)


def _dattn_kernel(lam_ref, q_ref, k_ref, v_ref, g_ref, o_ref, *, kc, rb, lam_init):
    tq = q_ref.shape[0]
    nkc = k_ref.shape[0] // kc
    nrb = tq // rb
    npan = rb // EXP_PANEL
    d = DA_HEAD_DIM
    lv = lam_ref[...]
    lam = (jnp.exp(jnp.sum(lv[0:1] * lv[1:2], axis=1, keepdims=True))
           - jnp.exp(jnp.sum(lv[2:3] * lv[3:4], axis=1, keepdims=True)) + lam_init)
    maps = (0, 1)

    def scores(r, c, mp, sv):
        for m in maps:
            kk = k_ref[c * kc:(c + 1) * kc, m * d:(m + 1) * d]
            s = lax.dot_general(q_ref[r * rb:(r + 1) * rb, m * d:(m + 1) * d], kk,
                                (((1,), (1,)), ((), ())), preferred_element_type=F32)
            sv[m, c] = s
            cm = s[:, 0:LANES]
            for t in range(1, kc // LANES):
                cm = jnp.maximum(cm, s[:, t * LANES:(t + 1) * LANES])
            mp[m] = cm if mp[m] is None else jnp.maximum(mp[m], cm)

    def exps(c, mb, lp, sv, pv):
        for m in maps:
            for i in range(npan):
                rows = slice(i * EXP_PANEL, (i + 1) * EXP_PANEL)
                cl = None
                for t in range(kc // LANES):
                    pt = jnp.exp(sv[m, c][rows, t * LANES:(t + 1) * LANES] - mb[m][i])
                    cl = pt if cl is None else cl + pt
                    pv[m, c, i, t] = pt.astype(BF16)
                lp[m][i] = cl if lp[m][i] is None else lp[m][i] + cl

    def weighted(r, lp, pv):
        hk = k_ref.shape[0] // 2
        heads = []
        for m in maps:
            p = jnp.concatenate(
                [jnp.concatenate(
                    [jnp.concatenate([pv[m, c, i, t] for t in range(kc // LANES)], axis=1)
                     for i in range(npan)], axis=0) for c in range(nkc)], axis=1)
            pvm = (jnp.dot(p[:, 0:hk], v_ref[0:hk, :], preferred_element_type=F32)
                   + jnp.dot(p[:, hk:], v_ref[hk:, :], preferred_element_type=F32))
            l = jnp.concatenate([jnp.sum(x, axis=1, keepdims=True) for x in lp[m]], axis=0)
            heads.append(pvm / l)
        o = heads[0] - lam * heads[1]
        ms = jnp.mean(o * o, axis=1, keepdims=True)
        o = o * lax.rsqrt(ms + EPS) * g_ref[...] * (1.0 - lam_init)
        o_ref[r * rb:(r + 1) * rb, :] = o.astype(o_ref.dtype)

    mp, mb, lp, sv, pv = {}, {}, {}, {}, {}
    for stage in range(nrb + 2):
        ra, re, rp = stage, stage - 1, stage - 2
        if 0 <= rp < nrb:
            weighted(rp, lp[rp], pv[rp])
        if 0 <= ra < nrb:
            mp[ra] = [None, None]
            sv[ra] = {}
        if 0 <= re < nrb:
            lp[re] = [[None] * npan for _ in maps]
            pv[re] = {}
        for c in range(nkc):
            if 0 <= re < nrb:
                exps(c, mb[re], lp[re], sv[re], pv[re])
            if 0 <= ra < nrb:
                scores(ra, c, mp[ra], sv[ra])
        if 0 <= ra < nrb:
            mb[ra] = []
            for m in maps:
                rowmax = jnp.max(mp[ra][m], axis=1, keepdims=True)
                mb[ra].append([jnp.broadcast_to(rowmax[i * EXP_PANEL:(i + 1) * EXP_PANEL],
                                                (EXP_PANEL, LANES)) for i in range(npan)])


def diff_attention(qkv, lam4, subln_g, layer, *, B, S, D, lam_init, tq=1024, rb=256, kc=512):
    T = B * S
    hw = 2 * DA_HEAD_DIM
    H = D // hw
    tq = min(tq, S)
    rb = min(rb, tq)
    nq = S // tq
    g3 = subln_g.reshape(subln_g.shape[0], 1, hw)
    kern = functools.partial(_dattn_kernel, kc=kc, rb=rb, lam_init=lam_init)
    return pl.pallas_call(
        kern,
        grid=(B, H, nq),
        in_specs=[pl.BlockSpec((None, 4, DA_HEAD_DIM), lambda b, h, i: (layer, 0, 0)),
                  pl.BlockSpec((tq, hw), lambda b, h, i: (b * nq + i, h)),
                  pl.BlockSpec((S, hw), lambda b, h, i: (b, H + h)),
                  pl.BlockSpec((S, hw), lambda b, h, i: (b, 2 * H + h)),
                  pl.BlockSpec((None, 1, hw), lambda b, h, i: (layer, 0, 0))],
        out_specs=pl.BlockSpec((tq, hw), lambda b, h, i: (b * nq + i, h)),
        out_shape=jax.ShapeDtypeStruct((T, D), BF16),
        compiler_params=_params("parallel", "parallel", "arbitrary"),
        name="diff_attention",
    )(lam4, qkv, qkv, qkv, g3)


def _xattn_kernel(q_ref, k_ref, v_ref, o_ref):
    hd = X_HEAD_DIM
    for h in range(X_HEADS):
        q = q_ref[:, h * hd:(h + 1) * hd]
        k = k_ref[:, h * hd:(h + 1) * hd]
        v = v_ref[:, h * hd:(h + 1) * hd]
        s = lax.dot_general(q, k, (((1,), (1,)), ((), ())), preferred_element_type=F32)
        p = jnp.exp(s - jnp.max(s, axis=1, keepdims=True))
        p = p / jnp.sum(p, axis=1, keepdims=True)
        o = jnp.dot(p.astype(BF16), v, preferred_element_type=F32)
        o_ref[:, h * hd:(h + 1) * hd] = o.astype(o_ref.dtype)


def xattn_core(q, kv, *, B, S, tm=512):
    T, XW = q.shape
    M = kv.shape[0] // B
    nt = S // tm
    return pl.pallas_call(
        _xattn_kernel,
        grid=(B, nt),
        in_specs=[pl.BlockSpec((tm, XW), lambda b, i: (b * nt + i, 0)),
                  pl.BlockSpec((M, XW), lambda b, i: (b, 0)),
                  pl.BlockSpec((M, XW), lambda b, i: (b, 1))],
        out_specs=pl.BlockSpec((tm, XW), lambda b, i: (b * nt + i, 0)),
        out_shape=jax.ShapeDtypeStruct((T, XW), BF16),
        compiler_params=_params("parallel", "parallel"),
        name="xattn_core",
    )(q, kv, kv)


def _softplus(x):
    return jnp.maximum(x, 0.0) + jnp.log1p(jnp.exp(-jnp.abs(x)))


def _sigmoid(x):
    return 1.0 / (1.0 + jnp.exp(-x))


def _gelu_tanh(g):
    c = math.sqrt(2.0 / math.pi)
    half = 0.5 * g
    return half + half * jnp.tanh(g * (c + (c * 0.044715) * (g * g)))


def _blocked_scan(a_ref, x_ref, h_ref, gp_ref, gh_ref, ent_ref, l3_ref, d3_ref, e3_ref,
                  h0, reverse):
    tc = a_ref.shape[0]
    g = tc // SUBLANES
    n3 = g // SUBLANES
    order = tuple(range(SUBLANES - 1, -1, -1)) if reverse else tuple(range(SUBLANES))
    korder = tuple(range(n3 - 1, -1, -1)) if reverse else tuple(range(n3))
    last = order[-1]

    def strided(ref, r, n):
        return ref[pl.ds(r, n, stride=SUBLANES), :]

    loc, dec = {}, {}
    prev = None
    for r in order:
        a, x = strided(a_ref, r, g), strided(x_ref, r, g)
        loc[r] = x if prev is None else a * loc[prev] + x
        dec[r] = a if prev is None else a * dec[prev]
        prev = r
    gp_ref[...] = dec[last]
    gh_ref[...] = loc[last]

    loc2, dec2 = {}, {}
    prev = None
    for s in order:
        q, k = strided(gp_ref, s, n3), strided(gh_ref, s, n3)
        loc2[s] = k if prev is None else q * loc2[prev] + k
        dec2[s] = q if prev is None else q * dec2[prev]
        prev = s

    state = h0
    l3_ref[...] = loc2[last]
    d3_ref[...] = dec2[last]
    for k in korder:
        e3_ref[k:k + 1, :] = state
        state = l3_ref[k:k + 1, :] + d3_ref[k:k + 1, :] * state
    entering3 = e3_ref[...]

    entering = entering3
    for s in order:
        ent_ref[pl.ds(s, n3, stride=SUBLANES), :] = entering
        entering = loc2[s] + dec2[s] * entering3
    ent = ent_ref[...]
    for r in order:
        h_ref[pl.ds(r, g, stride=SUBLANES), :] = loc[r] + dec[r] * ent
    return state


def _lru_kernel(u_ref, up_ref, un_ref, gate_ref, cw_ref, cb_ref,
                waf_ref, baf_ref, wif_ref, bif_ref, lamf_ref,
                wab_ref, bab_ref, wib_ref, bib_ref, lamb_ref,
                y_ref, pad_ref, a_ref, x_ref, h_ref, gp_ref, gh_ref, ent_ref,
                l3_ref, d3_ref, e3_ref, hf_ref, uc_ref, st_ref):
    ph = pl.program_id(2)
    c = pl.program_id(3)
    nt = pl.num_programs(3)
    cc = jnp.where(ph == 0, c, nt - 1 - c)
    tc, bw = u_ref.shape
    h = SUBLANES
    row0 = pl.multiple_of(cc * tc, tc)
    lane_tiles = [slice(l * LANES, (l + 1) * LANES) for l in range(bw // LANES)]

    @pl.when(c == 0)
    def _():
        st_ref[...] = jnp.zeros_like(st_ref)

    def conv():
        pad_ref[0:h] = jnp.where(cc > 0, up_ref[...], 0.0)
        pad_ref[h:h + tc] = u_ref[...]
        pad_ref[h + tc:2 * h + tc] = jnp.where(cc < nt - 1, un_ref[...], 0.0)
        cw = cw_ref[...]
        uc = cb_ref[...]
        for t in range(CONV_WIDTH):
            off = h + t - CONV_LEFT
            uc = uc + pad_ref[off:off + tc] * cw[t:t + 1]
        return uc

    def sweep(uc, wa, ba, wi, bi, lam, reverse):
        ub = uc.astype(BF16)

        def gate(w_ref, b_ref):
            z = jnp.dot(ub, w_ref[...].astype(BF16), preferred_element_type=F32) + b_ref[...]
            return _sigmoid(z)

        r = gate(wa, ba)
        ig = gate(wi, bi)
        decay = (LRU_C * _softplus(-lam[...])) * r
        a = jnp.exp(-decay)
        y = jnp.tanh(decay) * (1.0 + a * a)
        root = jnp.where(y > 0.0, y * lax.rsqrt(y), 0.0)
        x = root * (ig * uc)
        for l, lanes in enumerate(lane_tiles):
            a_ref[l] = a[:, lanes]
            x_ref[l] = x[:, lanes]
        for l, lanes in enumerate(lane_tiles):
            st_ref[0:1, lanes] = _blocked_scan(
                a_ref.at[l], x_ref.at[l], h_ref.at[l], gp_ref.at[l], gh_ref.at[l], ent_ref.at[l],
                l3_ref.at[l], d3_ref.at[l], e3_ref.at[l], st_ref[0:1, lanes], reverse)

    @pl.when(ph == 0)
    def _():
        uc = conv()
        uc_ref[pl.ds(row0, tc), :] = uc
        sweep(uc, waf_ref, baf_ref, wif_ref, bif_ref, lamf_ref, False)
        for l in range(len(lane_tiles)):
            hf_ref[l, pl.ds(row0, tc), :] = h_ref[l]

    @pl.when(ph == 1)
    def _():
        sweep(uc_ref[pl.ds(row0, tc), :], wab_ref, bab_ref, wib_ref, bib_ref, lamb_ref, True)
        for l, lanes in enumerate(lane_tiles):
            hsum = hf_ref[l, pl.ds(row0, tc), :] + h_ref[l]
            y_ref[:, lanes] = (hsum * _gelu_tanh(gate_ref[:, lanes])).astype(y_ref.dtype)


def rglru(ug, layer, conv_w, conv_b, dirs, *, B, S, W, tc=1024):
    T = B * S
    bw = LRU_BLOCK_W
    nb = W // bw
    tc = min(tc, S)
    nt = S // tc
    h = SUBLANES
    assert tc % (h * h) == 0 and S % tc == 0
    last8 = T // h - 1

    def early(p, c):
        return c + p * (nt - 1 - c)

    def late(p, c):
        return nt - 1 - p * c

    in_specs = [
        pl.BlockSpec((tc, bw), lambda n, b, p, c: (b * nt + early(p, c), n)),
        pl.BlockSpec((h, bw), lambda n, b, p, c: (
            jnp.maximum((b * S + early(p, c) * tc) // h - 1, 0), n)),
        pl.BlockSpec((h, bw), lambda n, b, p, c: (
            jnp.minimum((b * S + (early(p, c) + 1) * tc) // h, last8), n)),
        pl.BlockSpec((tc, bw), lambda n, b, p, c: (b * nt + late(p, c), nb + n)),
        pl.BlockSpec((None, CONV_WIDTH, bw), lambda n, b, p, c: (layer, 0, n)),
        pl.BlockSpec((None, 1, bw), lambda n, b, p, c: (layer, 0, n))]
    args = [ug, ug, ug, ug, conv_w, conv_b.reshape(conv_b.shape[0], 1, W)]
    wspec = pl.BlockSpec((None, None, bw, bw), lambda n, b, p, c: (layer, n, 0, 0))
    vspec = pl.BlockSpec((None, None, 1, bw), lambda n, b, p, c: (layer, n, 0, 0))
    for wa, ba, wi, bi, lam in dirs:
        L = wa.shape[0]
        in_specs += [wspec, vspec, wspec, vspec, vspec]
        args += [wa, ba.reshape(L, nb, 1, bw), wi, bi.reshape(L, nb, 1, bw),
                 lam.reshape(L, nb, 1, bw)]
    g = tc // h
    nl = bw // LANES

    def tiles(rows):
        return pltpu.VMEM((nl, rows, LANES), F32)

    return pl.pallas_call(
        _lru_kernel,
        grid=(nb, B, 2, nt),
        in_specs=in_specs,
        out_specs=pl.BlockSpec((tc, bw), lambda n, b, p, c: (b * nt + late(p, c), n)),
        out_shape=jax.ShapeDtypeStruct((T, W), BF16),
        scratch_shapes=[pltpu.VMEM((tc + 2 * h, bw), F32),
                        tiles(tc),
                        tiles(tc),
                        tiles(tc),
                        tiles(g),
                        tiles(g),
                        tiles(g),
                        tiles(g // h),
                        tiles(g // h),
                        tiles(g // h),
                        tiles(S),
                        pltpu.VMEM((S, bw), F32),
                        pltpu.VMEM((h, bw), F32)],
        compiler_params=_params("parallel", "parallel", "arbitrary", "arbitrary"),
        name="rglru",
    )(*args)


def _lambda_init(layer_idx):
    return 0.8 - 0.6 * math.exp(-0.3 * layer_idx)


def kernel(x, mem, positions, attn_norm_g, attn_w_qkv, attn_lambda_q1, attn_lambda_k1, attn_lambda_q2, attn_lambda_k2, attn_subln_g, attn_w_o, rnn_norm_g, rnn_w_in, rnn_conv_w, rnn_conv_b, rnn_wa_f, rnn_ba_f, rnn_wi_f, rnn_bi_f, rnn_lam_f, rnn_wa_b, rnn_ba_b, rnn_wi_b, rnn_bi_b, rnn_lam_b, rnn_w_out, xattn_norm_g, xattn_mem_g, xattn_w_q, xattn_w_kv, xattn_w_o, mlp_norm_g, mlp_w1, mlp_w2, final_g):
    B, S, D = x.shape
    T = B * S
    depth = xattn_norm_g.shape[0]
    M = mem.shape[1]
    W = rnn_lam_f.shape[-1]

    h = x.reshape(T, D)
    mem2 = mem.reshape(B * M, D)
    rope = rope_tables(positions)
    lam4 = jnp.stack([attn_lambda_q1, attn_lambda_k1, attn_lambda_q2, attn_lambda_k2], axis=1)

    stats = None

    def normed(w3, layer, g2, **kw):
        if stats is None:
            return matmul(rmsnorm(h, g2, layer, BF16), w3, layer, **kw)
        return matmul(stats[0], w3, layer, norm=(g2[layer], stats[1]), **kw)

    nslab = mlp_w2.shape[1] // min(MM_TK, mlp_w2.shape[1])
    for i in range(depth):
        j = i // N_MIXERS
        if i % N_MIXERS == 0:
            qkv = normed(attn_w_qkv, j, attn_norm_g, out_dtype=BF16,
                         scale=DA_HEAD_DIM ** -0.5, rope=rope, rope_width=D)
            o = diff_attention(qkv, lam4, attn_subln_g, j, B=B, S=S, D=D,
                               lam_init=_lambda_init(i))
            h, *stats = matmul(o, attn_w_o, j, out_dtype=F32, res=h, stats_out=True)
        else:
            ug = normed(rnn_w_in, j, rnn_norm_g, out_dtype=F32)
            y = rglru(ug, j, rnn_conv_w, rnn_conv_b,
                      ((rnn_wa_f, rnn_ba_f, rnn_wi_f, rnn_bi_f, rnn_lam_f),
                       (rnn_wa_b, rnn_ba_b, rnn_wi_b, rnn_bi_b, rnn_lam_b)),
                      B=B, S=S, W=W)
            h, *stats = matmul(y, rnn_w_out, j, out_dtype=F32, res=h, stats_out=True)

        q = normed(xattn_w_q, i, xattn_norm_g, out_dtype=BF16, scale=X_HEAD_DIM ** -0.5)
        kv = matmul(rmsnorm(mem2, xattn_mem_g, i, BF16), xattn_w_kv, i, out_dtype=BF16)
        o = xattn_core(q, kv, B=B, S=S)
        h, *stats = matmul(o, xattn_w_o, i, out_dtype=F32, res=h, tm=512, tn=2048,
                           stats_out=True)

        hid = normed(mlp_w1, i, mlp_norm_g, out_dtype=BF16, relu2=True)
        for ks in range(nslab):
            if ks == nslab - 1 and i < depth - 1:
                h, *stats = matmul(hid, mlp_w2, i, out_dtype=F32, res=h, kslab=ks, stats_out=True)
            else:
                h = matmul(hid, mlp_w2, i, out_dtype=F32, res=h, kslab=ks)

    return rmsnorm(h, final_g.reshape(1, D), 0, F32).reshape(B, S, D)
```

```python
import functools
import math

import jax
import jax.numpy as jnp
from jax import lax
from jax.experimental import pallas as pl
from jax.experimental.pallas import tpu as pltpu

F32 = jnp.float32
BF16 = jnp.bfloat16

EPS = 1e-6
ROPE_THETA = 500000.0
DA_HEAD_DIM = 128
ROT_DIM = DA_HEAD_DIM // 4
LRU_BLOCK_W = 256
LRU_C = 8.0
CONV_WIDTH = 4
CONV_LEFT = 2
X_HEADS = 4
X_HEAD_DIM = 128
N_MIXERS = 2

LANES = 128
SUBLANES = 8
VMEM_LIMIT_BYTES = 56 * 1024 * 1024
ATTN_VMEM_LIMIT_BYTES = 63 * 1024 * 1024
MM_TK = 4096
EXP_PANEL = 64


def _params(*semantics, vmem_limit_bytes=VMEM_LIMIT_BYTES):
    return pltpu.CompilerParams(dimension_semantics=semantics,
                                vmem_limit_bytes=vmem_limit_bytes)


def _rope_table_kernel(pos_ref, invf_ref, cos_ref, sa_ref, sb_ref):
    ang = pos_ref[...].astype(F32) * invf_ref[...]
    lane = lax.broadcasted_iota(jnp.int32, ang.shape, 1)
    half = ROT_DIM // 2
    c, s = jnp.cos(ang), jnp.sin(ang)
    cos_ref[...] = jnp.where(lane < ROT_DIM, c, 1.0)
    sa_ref[...] = jnp.where(lane < half, -s, 0.0)
    sb_ref[...] = jnp.where((lane >= half) & (lane < ROT_DIM), s, 0.0)


def rope_tables(positions, tm=1024):
    T = positions.size
    tm = min(tm, T)
    half = ROT_DIM // 2
    inv_freq = ROPE_THETA ** (-jnp.arange(0, ROT_DIM, 2, dtype=F32) / ROT_DIM)
    invf = jnp.tile(inv_freq, LANES // half)[None, :]
    tab = jax.ShapeDtypeStruct((T, LANES), F32)
    spec = pl.BlockSpec((tm, LANES), lambda i: (i, 0))
    return pl.pallas_call(
        _rope_table_kernel,
        grid=(T // tm,),
        in_specs=[pl.BlockSpec((tm, 1), lambda i: (i, 0)),
                  pl.BlockSpec((1, LANES), lambda i: (0, 0))],
        out_specs=[spec, spec, spec],
        out_shape=[tab, tab, tab],
        compiler_params=_params("parallel"),
        name="rope_tables",
    )(positions.reshape(T, 1), invf)


def _rmsnorm_kernel(x_ref, g_ref, o_ref):
    x = x_ref[...].astype(F32)
    ms = jnp.mean(x * x, axis=-1, keepdims=True)
    o_ref[...] = (x * lax.rsqrt(ms + EPS) * g_ref[...]).astype(o_ref.dtype)


def rmsnorm(x, g2, layer, out_dtype, tm=256):
    T, D = x.shape
    tm = min(tm, T)
    g3 = g2.reshape(g2.shape[0], 1, D)
    return pl.pallas_call(
        _rmsnorm_kernel,
        grid=(T // tm,),
        in_specs=[pl.BlockSpec((tm, D), lambda i: (i, 0)),
                  pl.BlockSpec((None, 1, D), lambda i: (layer, 0, 0))],
        out_specs=pl.BlockSpec((tm, D), lambda i: (i, 0)),
        out_shape=jax.ShapeDtypeStruct((T, D), out_dtype),
        compiler_params=_params("parallel"),
        name="rmsnorm",
    )(x, g3)


def _rope_cols(t, cos, sa, sb):
    half = ROT_DIM // 2
    return (t * cos + pltpu.roll(t, LANES - half, 1) * sa + pltpu.roll(t, half, 1) * sb)


def _mm_kernel(*refs, relu2, scale, has_res, rope_tiles, norm_dim, stats_out):
    it = iter(refs)
    a_ref, w_ref = next(it), next(it)
    res_ref = next(it) if has_res else None
    if rope_tiles:
        cos_ref, sa_ref, sb_ref = next(it), next(it), next(it)
    if norm_dim:
        g_ref, ssq_ref = next(it), next(it)
    o_ref = next(it)
    if stats_out:
        ob_ref, osq_ref = next(it), next(it)
    wbf_ref = next(it)
    if stats_out:
        acc_sq_ref = next(it)
    j, i = pl.program_id(0), pl.program_id(1)

    if stats_out:
        @pl.when(j == 0)
        def _():
            acc_sq_ref[i] = jnp.zeros(acc_sq_ref.shape[1:], F32)

    def tile(wb):
        acc = jnp.dot(a_ref[...], wb, preferred_element_type=F32)
        if norm_dim:
            ssq = jnp.sum(ssq_ref[...], axis=1, keepdims=True)
            acc = acc * lax.rsqrt(ssq * (1.0 / norm_dim) + EPS)

        if rope_tiles:
            rotary = j < 2 * rope_tiles
            qs = jnp.where(j < rope_tiles, scale, 1.0).astype(F32)
            cos = jnp.where(rotary, cos_ref[...], 1.0) * qs
            sa = jnp.where(rotary, sa_ref[...], 0.0) * qs
            sb = jnp.where(rotary, sb_ref[...], 0.0) * qs
            for c in range(acc.shape[1] // LANES):
                t = _rope_cols(acc[:, c * LANES:(c + 1) * LANES], cos, sa, sb)
                o_ref[:, c * LANES:(c + 1) * LANES] = t.astype(o_ref.dtype)
            return
        if relu2:
            r = jnp.maximum(acc, 0.0)
            acc = r * r
        if scale is not None:
            acc = acc * scale
        if has_res:
            acc = acc + res_ref[...]
        o_ref[...] = acc.astype(o_ref.dtype)
        if stats_out:
            ob_ref[...] = acc.astype(BF16)
            sq = acc * acc
            part = sq[:, 0:LANES]
            for c in range(1, acc.shape[1] // LANES):
                part = part + sq[:, c * LANES:(c + 1) * LANES]
            total = acc_sq_ref[i] + part
            acc_sq_ref[i] = total
            osq_ref[...] = total

    @pl.when(i == 0)
    def _():
        w = w_ref[...]
        if norm_dim:
            w = w * g_ref[...]
        wb = w.astype(BF16)
        wbf_ref[...] = wb
        tile(wb)

    @pl.when(i > 0)
    def _():
        tile(wbf_ref[...])


def matmul(a, w3, layer, *, out_dtype, kslab=0, tm=1024, tn=512, tk=MM_TK, relu2=False,
           scale=None, res=None, rope=None, rope_width=None, norm=None, stats_out=False):
    M = a.shape[0]
    _, K, N = w3.shape
    tm, tn, tk = min(tm, M), min(tn, N), min(tk, K)
    nj, ni = N // tn, M // tm
    in_specs = [pl.BlockSpec((tm, tk), lambda j, i: (i, kslab)),
                pl.BlockSpec((None, tk, tn), lambda j, i: (layer, kslab, j))]
    args = [a, w3]
    if res is not None:
        in_specs.append(pl.BlockSpec((tm, tn), lambda j, i: (i, j)))
        args.append(res)
    rope_tiles = 0
    if rope is not None:
        assert rope_width % tn == 0
        rope_tiles = rope_width // tn
        in_specs += [pl.BlockSpec((tm, LANES), lambda j, i: (i, 0))] * 3
        args += list(rope)
    if norm is not None:
        assert tk == K, "the folded rmsnorm needs the whole row in one contraction slab"
        g, ssq = norm
        in_specs += [pl.BlockSpec((tk, 1), lambda j, i: (0, 0)),
                     pl.BlockSpec((tm, LANES), lambda j, i: (i, 0))]
        args += [g.reshape(K, 1), ssq]
    out_specs = pl.BlockSpec((tm, tn), lambda j, i: (i, j))
    out_shape = jax.ShapeDtypeStruct((M, N), out_dtype)
    scratch = [pltpu.VMEM((tk, tn), BF16)]
    if stats_out:
        out_specs = [out_specs, pl.BlockSpec((tm, tn), lambda j, i: (i, j)),
                     pl.BlockSpec((tm, LANES), lambda j, i: (jnp.where(j == nj - 1, i, 0), 0))]
        out_shape = [out_shape, jax.ShapeDtypeStruct((M, N), BF16),
                     jax.ShapeDtypeStruct((M, LANES), F32)]
        scratch.append(pltpu.VMEM((ni, tm, LANES), F32))
    kern = functools.partial(_mm_kernel, relu2=relu2, scale=scale, has_res=res is not None,
                             rope_tiles=rope_tiles, norm_dim=K if norm is not None else 0,
                             stats_out=stats_out)
    return pl.pallas_call(
        kern,
        grid=(nj, ni),
        in_specs=in_specs,
        out_specs=out_specs,
        out_shape=out_shape,
        scratch_shapes=scratch,
        compiler_params=_params("arbitrary", "arbitrary"),
        name="matmul",
    )(*args)


def _dattn_kernel(lam_ref, q_ref, k_ref, v_ref, g_ref, o_ref, *, kc, rb, lam_init):
    tq = q_ref.shape[0]
    nkc = k_ref.shape[0] // kc
    nrb = tq // rb
    npan = rb // EXP_PANEL
    d = DA_HEAD_DIM
    lv = lam_ref[...]
    lam = (jnp.exp(jnp.sum(lv[0:1] * lv[1:2], axis=1, keepdims=True))
           - jnp.exp(jnp.sum(lv[2:3] * lv[3:4], axis=1, keepdims=True)) + lam_init)
    maps = (0, 1)

    def scores(r, c, mp, sv):
        for m in maps:
            kk = k_ref[c * kc:(c + 1) * kc, m * d:(m + 1) * d]
            s = lax.dot_general(q_ref[r * rb:(r + 1) * rb, m * d:(m + 1) * d], kk,
                                (((1,), (1,)), ((), ())), preferred_element_type=F32)
            sv[m, c] = s
            cm = s[:, 0:LANES]
            for t in range(1, kc // LANES):
                cm = jnp.maximum(cm, s[:, t * LANES:(t + 1) * LANES])
            mp[m] = cm if mp[m] is None else jnp.maximum(mp[m], cm)

    def exps(c, mb, lp, sv, pv):
        for m in maps:
            for i in range(npan):
                rows = slice(i * EXP_PANEL, (i + 1) * EXP_PANEL)
                cl = None
                for t in range(kc // LANES):
                    pt = jnp.exp(sv[m, c][rows, t * LANES:(t + 1) * LANES] - mb[m][i])
                    cl = pt if cl is None else cl + pt
                    pv[m, c, i, t] = pt.astype(BF16)
                lp[m][i] = cl if lp[m][i] is None else lp[m][i] + cl

    def weighted(r, lp, pv):
        hk = k_ref.shape[0] // 2
        heads = []
        for m in maps:
            p = jnp.concatenate(
                [jnp.concatenate(
                    [jnp.concatenate([pv[m, c, i, t] for t in range(kc // LANES)], axis=1)
                     for i in range(npan)], axis=0) for c in range(nkc)], axis=1)
            pvm = (jnp.dot(p[:, 0:hk], v_ref[0:hk, :], preferred_element_type=F32)
                   + jnp.dot(p[:, hk:], v_ref[hk:, :], preferred_element_type=F32))
            l = jnp.concatenate([jnp.sum(x, axis=1, keepdims=True) for x in lp[m]], axis=0)
            heads.append(pvm / l)
        o = heads[0] - lam * heads[1]
        ms = jnp.mean(o * o, axis=1, keepdims=True)
        o = o * lax.rsqrt(ms + EPS) * g_ref[...] * (1.0 - lam_init)
        o_ref[r * rb:(r + 1) * rb, :] = o.astype(o_ref.dtype)

    mp, mb, lp, sv, pv = {}, {}, {}, {}, {}
    for stage in range(nrb + 2):
        ra, re, rp = stage, stage - 1, stage - 2
        if 0 <= rp < nrb:
            weighted(rp, lp[rp], pv[rp])
        if 0 <= ra < nrb:
            mp[ra] = [None, None]
            sv[ra] = {}
        if 0 <= re < nrb:
            lp[re] = [[None] * npan for _ in maps]
            pv[re] = {}
        for c in range(nkc):
            if 0 <= re < nrb:
                exps(c, mb[re], lp[re], sv[re], pv[re])
            if 0 <= ra < nrb:
                scores(ra, c, mp[ra], sv[ra])
        if 0 <= ra < nrb:
            mb[ra] = []
            for m in maps:
                rowmax = jnp.max(mp[ra][m], axis=1, keepdims=True)
                mb[ra].append([jnp.broadcast_to(rowmax[i * EXP_PANEL:(i + 1) * EXP_PANEL],
                                                (EXP_PANEL, LANES)) for i in range(npan)])


def diff_attention(qkv, lam4, subln_g, layer, *, B, S, D, lam_init, tq=2048, rb=256, kc=512):
    T = B * S
    hw = 2 * DA_HEAD_DIM
    H = D // hw
    tq = min(tq, S)
    rb = min(rb, tq)
    nq = S // tq
    g3 = subln_g.reshape(subln_g.shape[0], 1, hw)
    kern = functools.partial(_dattn_kernel, kc=kc, rb=rb, lam_init=lam_init)
    return pl.pallas_call(
        kern,
        grid=(B, H, nq),
        in_specs=[pl.BlockSpec((None, 4, DA_HEAD_DIM), lambda b, h, i: (layer, 0, 0)),
                  pl.BlockSpec((tq, hw), lambda b, h, i: (b * nq + i, h)),
                  pl.BlockSpec((S, hw), lambda b, h, i: (b, H + h)),
                  pl.BlockSpec((S, hw), lambda b, h, i: (b, 2 * H + h)),
                  pl.BlockSpec((None, 1, hw), lambda b, h, i: (layer, 0, 0))],
        out_specs=pl.BlockSpec((tq, hw), lambda b, h, i: (b * nq + i, h)),
        out_shape=jax.ShapeDtypeStruct((T, D), BF16),
        compiler_params=_params("parallel", "parallel", "arbitrary",
                                vmem_limit_bytes=ATTN_VMEM_LIMIT_BYTES),
        name="diff_attention",
    )(lam4, qkv, qkv, qkv, g3)


def _xattn_kernel(q_ref, k_ref, v_ref, o_ref):
    hd = X_HEAD_DIM
    for h in range(X_HEADS):
        q = q_ref[:, h * hd:(h + 1) * hd]
        k = k_ref[:, h * hd:(h + 1) * hd]
        v = v_ref[:, h * hd:(h + 1) * hd]
        s = lax.dot_general(q, k, (((1,), (1,)), ((), ())), preferred_element_type=F32)
        p = jnp.exp(s - jnp.max(s, axis=1, keepdims=True))
        p = p / jnp.sum(p, axis=1, keepdims=True)
        o = jnp.dot(p.astype(BF16), v, preferred_element_type=F32)
        o_ref[:, h * hd:(h + 1) * hd] = o.astype(o_ref.dtype)


def xattn_core(q, kv, *, B, S, tm=512):
    T, XW = q.shape
    M = kv.shape[0] // B
    nt = S // tm
    return pl.pallas_call(
        _xattn_kernel,
        grid=(B, nt),
        in_specs=[pl.BlockSpec((tm, XW), lambda b, i: (b * nt + i, 0)),
                  pl.BlockSpec((M, XW), lambda b, i: (b, 0)),
                  pl.BlockSpec((M, XW), lambda b, i: (b, 1))],
        out_specs=pl.BlockSpec((tm, XW), lambda b, i: (b * nt + i, 0)),
        out_shape=jax.ShapeDtypeStruct((T, XW), BF16),
        compiler_params=_params("parallel", "parallel"),
        name="xattn_core",
    )(q, kv, kv)


def _softplus(x):
    return jnp.maximum(x, 0.0) + jnp.log1p(jnp.exp(-jnp.abs(x)))


def _sigmoid(x):
    return 1.0 / (1.0 + jnp.exp(-x))


def _gelu_tanh(g):
    c = math.sqrt(2.0 / math.pi)
    half = 0.5 * g
    return half + half * jnp.tanh(g * (c + (c * 0.044715) * (g * g)))


def _blocked_scan(a_ref, x_ref, h_ref, gp_ref, gh_ref, ent_ref, l3_ref, d3_ref, e3_ref,
                  h0, reverse):
    tc = a_ref.shape[0]
    g = tc // SUBLANES
    n3 = g // SUBLANES
    order = tuple(range(SUBLANES - 1, -1, -1)) if reverse else tuple(range(SUBLANES))
    korder = tuple(range(n3 - 1, -1, -1)) if reverse else tuple(range(n3))
    last = order[-1]

    def strided(ref, r, n):
        return ref[pl.ds(r, n, stride=SUBLANES), :]

    loc, dec = {}, {}
    prev = None
    for r in order:
        a, x = strided(a_ref, r, g), strided(x_ref, r, g)
        loc[r] = x if prev is None else a * loc[prev] + x
        dec[r] = a if prev is None else a * dec[prev]
        prev = r
    gp_ref[...] = dec[last]
    gh_ref[...] = loc[last]

    loc2, dec2 = {}, {}
    prev = None
    for s in order:
        q, k = strided(gp_ref, s, n3), strided(gh_ref, s, n3)
        loc2[s] = k if prev is None else q * loc2[prev] + k
        dec2[s] = q if prev is None else q * dec2[prev]
        prev = s

    state = h0
    l3_ref[...] = loc2[last]
    d3_ref[...] = dec2[last]
    for k in korder:
        e3_ref[k:k + 1, :] = state
        state = l3_ref[k:k + 1, :] + d3_ref[k:k + 1, :] * state
    entering3 = e3_ref[...]

    entering = entering3
    for s in order:
        ent_ref[pl.ds(s, n3, stride=SUBLANES), :] = entering
        entering = loc2[s] + dec2[s] * entering3
    ent = ent_ref[...]
    for r in order:
        h_ref[pl.ds(r, g, stride=SUBLANES), :] = loc[r] + dec[r] * ent
    return state


def _lru_kernel(u_ref, up_ref, un_ref, gate_ref, cw_ref, cb_ref,
                waf_ref, baf_ref, wif_ref, bif_ref, lamf_ref,
                wab_ref, bab_ref, wib_ref, bib_ref, lamb_ref,
                y_ref, pad_ref, a_ref, x_ref, h_ref, gp_ref, gh_ref, ent_ref,
                l3_ref, d3_ref, e3_ref, hf_ref, uc_ref, st_ref):
    ph = pl.program_id(2)
    c = pl.program_id(3)
    nt = pl.num_programs(3)
    cc = jnp.where(ph == 0, c, nt - 1 - c)
    tc, bw = u_ref.shape
    h = SUBLANES
    row0 = pl.multiple_of(cc * tc, tc)
    lane_tiles = [slice(l * LANES, (l + 1) * LANES) for l in range(bw // LANES)]

    @pl.when(c == 0)
    def _():
        st_ref[...] = jnp.zeros_like(st_ref)

    def conv():
        pad_ref[0:h] = jnp.where(cc > 0, up_ref[...], 0.0)
        pad_ref[h:h + tc] = u_ref[...]
        pad_ref[h + tc:2 * h + tc] = jnp.where(cc < nt - 1, un_ref[...], 0.0)
        cw = cw_ref[...]
        uc = cb_ref[...]
        for t in range(CONV_WIDTH):
            off = h + t - CONV_LEFT
            uc = uc + pad_ref[off:off + tc] * cw[t:t + 1]
        return uc

    def sweep(uc, wa, ba, wi, bi, lam, reverse):
        ub = uc.astype(BF16)

        def gate(w_ref, b_ref):
            z = jnp.dot(ub, w_ref[...].astype(BF16), preferred_element_type=F32) + b_ref[...]
            return _sigmoid(z)

        r = gate(wa, ba)
        ig = gate(wi, bi)
        decay = (LRU_C * _softplus(-lam[...])) * r
        a = jnp.exp(-decay)
        y = jnp.tanh(decay) * (1.0 + a * a)
        root = jnp.where(y > 0.0, y * lax.rsqrt(y), 0.0)
        x = root * (ig * uc)
        for l, lanes in enumerate(lane_tiles):
            a_ref[l] = a[:, lanes]
            x_ref[l] = x[:, lanes]
        for l, lanes in enumerate(lane_tiles):
            st_ref[0:1, lanes] = _blocked_scan(
                a_ref.at[l], x_ref.at[l], h_ref.at[l], gp_ref.at[l], gh_ref.at[l], ent_ref.at[l],
                l3_ref.at[l], d3_ref.at[l], e3_ref.at[l], st_ref[0:1, lanes], reverse)

    @pl.when(ph == 0)
    def _():
        uc = conv()
        uc_ref[pl.ds(row0, tc), :] = uc
        sweep(uc, waf_ref, baf_ref, wif_ref, bif_ref, lamf_ref, False)
        for l in range(len(lane_tiles)):
            hf_ref[l, pl.ds(row0, tc), :] = h_ref[l]

    @pl.when(ph == 1)
    def _():
        sweep(uc_ref[pl.ds(row0, tc), :], wab_ref, bab_ref, wib_ref, bib_ref, lamb_ref, True)
        for l, lanes in enumerate(lane_tiles):
            hsum = hf_ref[l, pl.ds(row0, tc), :] + h_ref[l]
            y_ref[:, lanes] = (hsum * _gelu_tanh(gate_ref[:, lanes])).astype(y_ref.dtype)


def rglru(ug, layer, conv_w, conv_b, dirs, *, B, S, W, tc=1024):
    T = B * S
    bw = LRU_BLOCK_W
    nb = W // bw
    tc = min(tc, S)
    nt = S // tc
    h = SUBLANES
    assert tc % (h * h) == 0 and S % tc == 0
    last8 = T // h - 1

    def early(p, c):
        return c + p * (nt - 1 - c)

    def late(p, c):
        return nt - 1 - p * c

    in_specs = [
        pl.BlockSpec((tc, bw), lambda n, b, p, c: (b * nt + early(p, c), n)),
        pl.BlockSpec((h, bw), lambda n, b, p, c: (
            jnp.maximum((b * S + early(p, c) * tc) // h - 1, 0), n)),
        pl.BlockSpec((h, bw), lambda n, b, p, c: (
            jnp.minimum((b * S + (early(p, c) + 1) * tc) // h, last8), n)),
        pl.BlockSpec((tc, bw), lambda n, b, p, c: (b * nt + late(p, c), nb + n)),
        pl.BlockSpec((None, CONV_WIDTH, bw), lambda n, b, p, c: (layer, 0, n)),
        pl.BlockSpec((None, 1, bw), lambda n, b, p, c: (layer, 0, n))]
    args = [ug, ug, ug, ug, conv_w, conv_b.reshape(conv_b.shape[0], 1, W)]
    wspec = pl.BlockSpec((None, None, bw, bw), lambda n, b, p, c: (layer, n, 0, 0))
    vspec = pl.BlockSpec((None, None, 1, bw), lambda n, b, p, c: (layer, n, 0, 0))
    for wa, ba, wi, bi, lam in dirs:
        L = wa.shape[0]
        in_specs += [wspec, vspec, wspec, vspec, vspec]
        args += [wa, ba.reshape(L, nb, 1, bw), wi, bi.reshape(L, nb, 1, bw),
                 lam.reshape(L, nb, 1, bw)]
    g = tc // h
    nl = bw // LANES

    def tiles(rows):
        return pltpu.VMEM((nl, rows, LANES), F32)

    return pl.pallas_call(
        _lru_kernel,
        grid=(nb, B, 2, nt),
        in_specs=in_specs,
        out_specs=pl.BlockSpec((tc, bw), lambda n, b, p, c: (b * nt + late(p, c), n)),
        out_shape=jax.ShapeDtypeStruct((T, W), BF16),
        scratch_shapes=[pltpu.VMEM((tc + 2 * h, bw), F32),
                        tiles(tc),
                        tiles(tc),
                        tiles(tc),
                        tiles(g),
                        tiles(g),
                        tiles(g),
                        tiles(g // h),
                        tiles(g // h),
                        tiles(g // h),
                        tiles(S),
                        pltpu.VMEM((S, bw), F32),
                        pltpu.VMEM((h, bw), F32)],
        compiler_params=_params("parallel", "parallel", "arbitrary", "arbitrary"),
        name="rglru",
    )(*args)


def _lambda_init(layer_idx):
    return 0.8 - 0.6 * math.exp(-0.3 * layer_idx)


def kernel(x, mem, positions, attn_norm_g, attn_w_qkv, attn_lambda_q1, attn_lambda_k1, attn_lambda_q2, attn_lambda_k2, attn_subln_g, attn_w_o, rnn_norm_g, rnn_w_in, rnn_conv_w, rnn_conv_b, rnn_wa_f, rnn_ba_f, rnn_wi_f, rnn_bi_f, rnn_lam_f, rnn_wa_b, rnn_ba_b, rnn_wi_b, rnn_bi_b, rnn_lam_b, rnn_w_out, xattn_norm_g, xattn_mem_g, xattn_w_q, xattn_w_kv, xattn_w_o, mlp_norm_g, mlp_w1, mlp_w2, final_g):
    B, S, D = x.shape
    T = B * S
    depth = xattn_norm_g.shape[0]
    M = mem.shape[1]
    W = rnn_lam_f.shape[-1]

    h = x.reshape(T, D)
    mem2 = mem.reshape(B * M, D)
    rope = rope_tables(positions)
    lam4 = jnp.stack([attn_lambda_q1, attn_lambda_k1, attn_lambda_q2, attn_lambda_k2], axis=1)

    stats = None

    def normed(w3, layer, g2, **kw):
        if stats is None:
            return matmul(rmsnorm(h, g2, layer, BF16), w3, layer, **kw)
        return matmul(stats[0], w3, layer, norm=(g2[layer], stats[1]), **kw)

    nslab = mlp_w2.shape[1] // min(MM_TK, mlp_w2.shape[1])
    for i in range(depth):
        j = i // N_MIXERS
        if i % N_MIXERS == 0:
            qkv = normed(attn_w_qkv, j, attn_norm_g, out_dtype=BF16,
                         scale=DA_HEAD_DIM ** -0.5, rope=rope, rope_width=D)
            o = diff_attention(qkv, lam4, attn_subln_g, j, B=B, S=S, D=D,
                               lam_init=_lambda_init(i))
            h, *stats = matmul(o, attn_w_o, j, out_dtype=F32, res=h, stats_out=True)
        else:
            ug = normed(rnn_w_in, j, rnn_norm_g, out_dtype=F32)
            y = rglru(ug, j, rnn_conv_w, rnn_conv_b,
                      ((rnn_wa_f, rnn_ba_f, rnn_wi_f, rnn_bi_f, rnn_lam_f),
                       (rnn_wa_b, rnn_ba_b, rnn_wi_b, rnn_bi_b, rnn_lam_b)),
                      B=B, S=S, W=W)
            h, *stats = matmul(y, rnn_w_out, j, out_dtype=F32, res=h, stats_out=True)

        q = normed(xattn_w_q, i, xattn_norm_g, out_dtype=BF16, scale=X_HEAD_DIM ** -0.5)
        kv = matmul(rmsnorm(mem2, xattn_mem_g, i, BF16), xattn_w_kv, i, out_dtype=BF16)
        o = xattn_core(q, kv, B=B, S=S)
        h, *stats = matmul(o, xattn_w_o, i, out_dtype=F32, res=h, tm=512, tn=2048,
                           stats_out=True)

        hid = normed(mlp_w1, i, mlp_norm_g, out_dtype=BF16, relu2=True)
        for ks in range(nslab):
            if ks == nslab - 1 and i < depth - 1:
                h, *stats = matmul(hid, mlp_w2, i, out_dtype=F32, res=h, kslab=ks, stats_out=True)
            else:
                h = matmul(hid, mlp_w2, i, out_dtype=F32, res=h, kslab=ks)

    return rmsnorm(h, final_g.reshape(1, D), 0, F32).reshape(B, S, D)
```

```python
import functools
import math

import jax
import jax.numpy as jnp
from jax import lax
from jax.experimental import pallas as pl
from jax.experimental.pallas import tpu as pltpu

F32 = jnp.float32
BF16 = jnp.bfloat16

EPS = 1e-6
ROPE_THETA = 500000.0
DA_HEAD_DIM = 128
ROT_DIM = DA_HEAD_DIM // 4
LRU_BLOCK_W = 256
LRU_C = 8.0
CONV_WIDTH = 4
CONV_LEFT = 2
X_HEADS = 4
X_HEAD_DIM = 128
N_MIXERS = 2

LANES = 128
SUBLANES = 8
VMEM_LIMIT_BYTES = 56 * 1024 * 1024
ATTN_VMEM_LIMIT_BYTES = 63 * 1024 * 1024
QKV_VMEM_LIMIT_BYTES = 60 * 1024 * 1024
MM_TK = 4096
EXP_PANEL = 64


def _params(*semantics, vmem_limit_bytes=VMEM_LIMIT_BYTES):
    return pltpu.CompilerParams(dimension_semantics=semantics,
                                vmem_limit_bytes=vmem_limit_bytes)


def _rope_table_kernel(pos_ref, invf_ref, cos_ref, sa_ref, sb_ref):
    ang = pos_ref[...].astype(F32) * invf_ref[...]
    lane = lax.broadcasted_iota(jnp.int32, ang.shape, 1)
    half = ROT_DIM // 2
    c, s = jnp.cos(ang), jnp.sin(ang)
    cos_ref[...] = jnp.where(lane < ROT_DIM, c, 1.0)
    sa_ref[...] = jnp.where(lane < half, -s, 0.0)
    sb_ref[...] = jnp.where((lane >= half) & (lane < ROT_DIM), s, 0.0)


def rope_tables(positions, tm=1024):
    T = positions.size
    tm = min(tm, T)
    half = ROT_DIM // 2
    inv_freq = ROPE_THETA ** (-jnp.arange(0, ROT_DIM, 2, dtype=F32) / ROT_DIM)
    invf = jnp.tile(inv_freq, LANES // half)[None, :]
    tab = jax.ShapeDtypeStruct((T, LANES), F32)
    spec = pl.BlockSpec((tm, LANES), lambda i: (i, 0))
    return pl.pallas_call(
        _rope_table_kernel,
        grid=(T // tm,),
        in_specs=[pl.BlockSpec((tm, 1), lambda i: (i, 0)),
                  pl.BlockSpec((1, LANES), lambda i: (0, 0))],
        out_specs=[spec, spec, spec],
        out_shape=[tab, tab, tab],
        compiler_params=_params("parallel"),
        name="rope_tables",
    )(positions.reshape(T, 1), invf)


def _rmsnorm_kernel(x_ref, g_ref, o_ref):
    x = x_ref[...].astype(F32)
    ms = jnp.mean(x * x, axis=-1, keepdims=True)
    o_ref[...] = (x * lax.rsqrt(ms + EPS) * g_ref[...]).astype(o_ref.dtype)


def rmsnorm(x, g2, layer, out_dtype, tm=256):
    T, D = x.shape
    tm = min(tm, T)
    g3 = g2.reshape(g2.shape[0], 1, D)
    return pl.pallas_call(
        _rmsnorm_kernel,
        grid=(T // tm,),
        in_specs=[pl.BlockSpec((tm, D), lambda i: (i, 0)),
                  pl.BlockSpec((None, 1, D), lambda i: (layer, 0, 0))],
        out_specs=pl.BlockSpec((tm, D), lambda i: (i, 0)),
        out_shape=jax.ShapeDtypeStruct((T, D), out_dtype),
        compiler_params=_params("parallel"),
        name="rmsnorm",
    )(x, g3)


def _rope_cols(t, cos, sa, sb):
    half = ROT_DIM // 2
    return (t * cos + pltpu.roll(t, LANES - half, 1) * sa + pltpu.roll(t, half, 1) * sb)


def _mm_kernel(*refs, relu2, scale, has_res, rope_tiles, norm_dim, stats_out):
    it = iter(refs)
    a_ref, w_ref = next(it), next(it)
    res_ref = next(it) if has_res else None
    if rope_tiles:
        cos_ref, sa_ref, sb_ref = next(it), next(it), next(it)
    if norm_dim:
        g_ref, ssq_ref = next(it), next(it)
    o_ref = next(it)
    if stats_out:
        ob_ref, osq_ref = next(it), next(it)
    wbf_ref = next(it)
    if stats_out:
        acc_sq_ref = next(it)
    j, i = pl.program_id(0), pl.program_id(1)

    if stats_out:
        @pl.when(j == 0)
        def _():
            acc_sq_ref[i] = jnp.zeros(acc_sq_ref.shape[1:], F32)

    def tile(wb):
        acc = jnp.dot(a_ref[...], wb, preferred_element_type=F32)
        if norm_dim:
            ssq = jnp.sum(ssq_ref[...], axis=1, keepdims=True)
            acc = acc * lax.rsqrt(ssq * (1.0 / norm_dim) + EPS)

        if rope_tiles:
            rotary = j < 2 * rope_tiles
            qs = jnp.where(j < rope_tiles, scale, 1.0).astype(F32)
            cos = jnp.where(rotary, cos_ref[...], 1.0) * qs
            sa = jnp.where(rotary, sa_ref[...], 0.0) * qs
            sb = jnp.where(rotary, sb_ref[...], 0.0) * qs
            for c in range(acc.shape[1] // LANES):
                t = _rope_cols(acc[:, c * LANES:(c + 1) * LANES], cos, sa, sb)
                o_ref[:, c * LANES:(c + 1) * LANES] = t.astype(o_ref.dtype)
            return
        if relu2:
            r = jnp.maximum(acc, 0.0)
            acc = r * r
        if scale is not None:
            acc = acc * scale
        if has_res:
            acc = acc + res_ref[...]
        o_ref[...] = acc.astype(o_ref.dtype)
        if stats_out:
            ob_ref[...] = acc.astype(BF16)
            sq = acc * acc
            part = sq[:, 0:LANES]
            for c in range(1, acc.shape[1] // LANES):
                part = part + sq[:, c * LANES:(c + 1) * LANES]
            total = acc_sq_ref[i] + part
            acc_sq_ref[i] = total
            osq_ref[...] = total

    @pl.when(i == 0)
    def _():
        w = w_ref[...]
        if norm_dim:
            w = w * g_ref[...]
        wb = w.astype(BF16)
        wbf_ref[...] = wb
        tile(wb)

    @pl.when(i > 0)
    def _():
        tile(wbf_ref[...])


def matmul(a, w3, layer, *, out_dtype, kslab=0, tm=1024, tn=512, tk=MM_TK, relu2=False,
           scale=None, res=None, rope=None, rope_width=None, norm=None, stats_out=False,
           vmem_limit_bytes=VMEM_LIMIT_BYTES):
    M = a.shape[0]
    _, K, N = w3.shape
    tm, tn, tk = min(tm, M), min(tn, N), min(tk, K)
    if rope is not None:
        tn = min(tn, rope_width)
    nj, ni = N // tn, M // tm
    in_specs = [pl.BlockSpec((tm, tk), lambda j, i: (i, kslab)),
                pl.BlockSpec((None, tk, tn), lambda j, i: (layer, kslab, j))]
    args = [a, w3]
    if res is not None:
        in_specs.append(pl.BlockSpec((tm, tn), lambda j, i: (i, j)))
        args.append(res)
    rope_tiles = 0
    if rope is not None:
        assert rope_width % tn == 0
        rope_tiles = rope_width // tn
        in_specs += [pl.BlockSpec((tm, LANES), lambda j, i: (i, 0))] * 3
        args += list(rope)
    if norm is not None:
        assert tk == K, "the folded rmsnorm needs the whole row in one contraction slab"
        g, ssq = norm
        in_specs += [pl.BlockSpec((tk, 1), lambda j, i: (0, 0)),
                     pl.BlockSpec((tm, LANES), lambda j, i: (i, 0))]
        args += [g.reshape(K, 1), ssq]
    out_specs = pl.BlockSpec((tm, tn), lambda j, i: (i, j))
    out_shape = jax.ShapeDtypeStruct((M, N), out_dtype)
    scratch = [pltpu.VMEM((tk, tn), BF16)]
    if stats_out:
        out_specs = [out_specs, pl.BlockSpec((tm, tn), lambda j, i: (i, j)),
                     pl.BlockSpec((tm, LANES), lambda j, i: (jnp.where(j == nj - 1, i, 0), 0))]
        out_shape = [out_shape, jax.ShapeDtypeStruct((M, N), BF16),
                     jax.ShapeDtypeStruct((M, LANES), F32)]
        scratch.append(pltpu.VMEM((ni, tm, LANES), F32))
    kern = functools.partial(_mm_kernel, relu2=relu2, scale=scale, has_res=res is not None,
                             rope_tiles=rope_tiles, norm_dim=K if norm is not None else 0,
                             stats_out=stats_out)
    return pl.pallas_call(
        kern,
        grid=(nj, ni),
        in_specs=in_specs,
        out_specs=out_specs,
        out_shape=out_shape,
        scratch_shapes=scratch,
        compiler_params=_params("arbitrary", "arbitrary", vmem_limit_bytes=vmem_limit_bytes),
        name="matmul",
    )(*args)


def _dattn_kernel(lam_ref, q_ref, k_ref, v_ref, g_ref, o_ref, *, kc, rb, lam_init):
    tq = q_ref.shape[0]
    nkc = k_ref.shape[0] // kc
    nrb = tq // rb
    npan = rb // EXP_PANEL
    d = DA_HEAD_DIM
    lv = lam_ref[...]
    lam = (jnp.exp(jnp.sum(lv[0:1] * lv[1:2], axis=1, keepdims=True))
           - jnp.exp(jnp.sum(lv[2:3] * lv[3:4], axis=1, keepdims=True)) + lam_init)
    maps = (0, 1)

    def scores(r, c, mp, sv):
        for m in maps:
            kk = k_ref[c * kc:(c + 1) * kc, m * d:(m + 1) * d]
            s = lax.dot_general(q_ref[r * rb:(r + 1) * rb, m * d:(m + 1) * d], kk,
                                (((1,), (1,)), ((), ())), preferred_element_type=F32)
            sv[m, c] = s
            cm = s[:, 0:LANES]
            for t in range(1, kc // LANES):
                cm = jnp.maximum(cm, s[:, t * LANES:(t + 1) * LANES])
            mp[m] = cm if mp[m] is None else jnp.maximum(mp[m], cm)

    def exps(c, mb, lp, sv, pv):
        for m in maps:
            for i in range(npan):
                rows = slice(i * EXP_PANEL, (i + 1) * EXP_PANEL)
                cl = None
                for t in range(kc // LANES):
                    pt = jnp.exp(sv[m, c][rows, t * LANES:(t + 1) * LANES] - mb[m][i])
                    cl = pt if cl is None else cl + pt
                    pv[m, c, i, t] = pt.astype(BF16)
                lp[m][i] = cl if lp[m][i] is None else lp[m][i] + cl

    def weighted(r, lp, pv):
        hk = k_ref.shape[0] // 2
        heads = []
        for m in maps:
            p = jnp.concatenate(
                [jnp.concatenate(
                    [jnp.concatenate([pv[m, c, i, t] for t in range(kc // LANES)], axis=1)
                     for i in range(npan)], axis=0) for c in range(nkc)], axis=1)
            pvm = (jnp.dot(p[:, 0:hk], v_ref[0:hk, :], preferred_element_type=F32)
                   + jnp.dot(p[:, hk:], v_ref[hk:, :], preferred_element_type=F32))
            l = jnp.concatenate([jnp.sum(x, axis=1, keepdims=True) for x in lp[m]], axis=0)
            heads.append(pvm / l)
        o = heads[0] - lam * heads[1]
        ms = jnp.mean(o * o, axis=1, keepdims=True)
        o = o * lax.rsqrt(ms + EPS) * g_ref[...] * (1.0 - lam_init)
        o_ref[r * rb:(r + 1) * rb, :] = o.astype(o_ref.dtype)

    mp, mb, lp, sv, pv = {}, {}, {}, {}, {}
    for stage in range(nrb + 2):
        ra, re, rp = stage, stage - 1, stage - 2
        if 0 <= rp < nrb:
            weighted(rp, lp[rp], pv[rp])
        if 0 <= ra < nrb:
            mp[ra] = [None, None]
            sv[ra] = {}
        if 0 <= re < nrb:
            lp[re] = [[None] * npan for _ in maps]
            pv[re] = {}
        for c in range(nkc):
            if 0 <= re < nrb:
                exps(c, mb[re], lp[re], sv[re], pv[re])
            if 0 <= ra < nrb:
                scores(ra, c, mp[ra], sv[ra])
        if 0 <= ra < nrb:
            mb[ra] = []
            for m in maps:
                rowmax = jnp.max(mp[ra][m], axis=1, keepdims=True)
                mb[ra].append([jnp.broadcast_to(rowmax[i * EXP_PANEL:(i + 1) * EXP_PANEL],
                                                (EXP_PANEL, LANES)) for i in range(npan)])


def diff_attention(qkv, lam4, subln_g, layer, *, B, S, D, lam_init, tq=2048, rb=256, kc=512):
    T = B * S
    hw = 2 * DA_HEAD_DIM
    H = D // hw
    tq = min(tq, S)
    rb = min(rb, tq)
    nq = S // tq
    g3 = subln_g.reshape(subln_g.shape[0], 1, hw)
    kern = functools.partial(_dattn_kernel, kc=kc, rb=rb, lam_init=lam_init)
    return pl.pallas_call(
        kern,
        grid=(B, H, nq),
        in_specs=[pl.BlockSpec((None, 4, DA_HEAD_DIM), lambda b, h, i: (layer, 0, 0)),
                  pl.BlockSpec((tq, hw), lambda b, h, i: (b * nq + i, h)),
                  pl.BlockSpec((S, hw), lambda b, h, i: (b, H + h)),
                  pl.BlockSpec((S, hw), lambda b, h, i: (b, 2 * H + h)),
                  pl.BlockSpec((None, 1, hw), lambda b, h, i: (layer, 0, 0))],
        out_specs=pl.BlockSpec((tq, hw), lambda b, h, i: (b * nq + i, h)),
        out_shape=jax.ShapeDtypeStruct((T, D), BF16),
        compiler_params=_params("parallel", "parallel", "arbitrary",
                                vmem_limit_bytes=ATTN_VMEM_LIMIT_BYTES),
        name="diff_attention",
    )(lam4, qkv, qkv, qkv, g3)


def _xattn_kernel(q_ref, k_ref, v_ref, o_ref):
    hd = X_HEAD_DIM
    for h in range(X_HEADS):
        q = q_ref[:, h * hd:(h + 1) * hd]
        k = k_ref[:, h * hd:(h + 1) * hd]
        v = v_ref[:, h * hd:(h + 1) * hd]
        s = lax.dot_general(q, k, (((1,), (1,)), ((), ())), preferred_element_type=F32)
        p = jnp.exp(s - jnp.max(s, axis=1, keepdims=True))
        p = p / jnp.sum(p, axis=1, keepdims=True)
        o = jnp.dot(p.astype(BF16), v, preferred_element_type=F32)
        o_ref[:, h * hd:(h + 1) * hd] = o.astype(o_ref.dtype)


def xattn_core(q, kv, *, B, S, tm=512):
    T, XW = q.shape
    M = kv.shape[0] // B
    nt = S // tm
    return pl.pallas_call(
        _xattn_kernel,
        grid=(B, nt),
        in_specs=[pl.BlockSpec((tm, XW), lambda b, i: (b * nt + i, 0)),
                  pl.BlockSpec((M, XW), lambda b, i: (b, 0)),
                  pl.BlockSpec((M, XW), lambda b, i: (b, 1))],
        out_specs=pl.BlockSpec((tm, XW), lambda b, i: (b * nt + i, 0)),
        out_shape=jax.ShapeDtypeStruct((T, XW), BF16),
        compiler_params=_params("parallel", "parallel"),
        name="xattn_core",
    )(q, kv, kv)


def _softplus(x):
    return jnp.maximum(x, 0.0) + jnp.log1p(jnp.exp(-jnp.abs(x)))


def _sigmoid(x):
    return 1.0 / (1.0 + jnp.exp(-x))


def _gelu_tanh(g):
    c = math.sqrt(2.0 / math.pi)
    half = 0.5 * g
    return half + half * jnp.tanh(g * (c + (c * 0.044715) * (g * g)))


def _blocked_scan(a_ref, x_ref, h_ref, gp_ref, gh_ref, ent_ref, l3_ref, d3_ref, e3_ref,
                  h0, reverse):
    tc = a_ref.shape[0]
    g = tc // SUBLANES
    n3 = g // SUBLANES
    order = tuple(range(SUBLANES - 1, -1, -1)) if reverse else tuple(range(SUBLANES))
    korder = tuple(range(n3 - 1, -1, -1)) if reverse else tuple(range(n3))
    last = order[-1]

    def strided(ref, r, n):
        return ref[pl.ds(r, n, stride=SUBLANES), :]

    loc, dec = {}, {}
    prev = None
    for r in order:
        a, x = strided(a_ref, r, g), strided(x_ref, r, g)
        loc[r] = x if prev is None else a * loc[prev] + x
        dec[r] = a if prev is None else a * dec[prev]
        prev = r
    gp_ref[...] = dec[last]
    gh_ref[...] = loc[last]

    loc2, dec2 = {}, {}
    prev = None
    for s in order:
        q, k = strided(gp_ref, s, n3), strided(gh_ref, s, n3)
        loc2[s] = k if prev is None else q * loc2[prev] + k
        dec2[s] = q if prev is None else q * dec2[prev]
        prev = s

    state = h0
    l3_ref[...] = loc2[last]
    d3_ref[...] = dec2[last]
    for k in korder:
        e3_ref[k:k + 1, :] = state
        state = l3_ref[k:k + 1, :] + d3_ref[k:k + 1, :] * state
    entering3 = e3_ref[...]

    entering = entering3
    for s in order:
        ent_ref[pl.ds(s, n3, stride=SUBLANES), :] = entering
        entering = loc2[s] + dec2[s] * entering3
    ent = ent_ref[...]
    for r in order:
        h_ref[pl.ds(r, g, stride=SUBLANES), :] = loc[r] + dec[r] * ent
    return state


def _lru_kernel(u_ref, up_ref, un_ref, gate_ref, cw_ref, cb_ref,
                waf_ref, baf_ref, wif_ref, bif_ref, lamf_ref,
                wab_ref, bab_ref, wib_ref, bib_ref, lamb_ref,
                y_ref, pad_ref, a_ref, x_ref, h_ref, gp_ref, gh_ref, ent_ref,
                l3_ref, d3_ref, e3_ref, hf_ref, uc_ref, st_ref):
    ph = pl.program_id(2)
    c = pl.program_id(3)
    nt = pl.num_programs(3)
    cc = jnp.where(ph == 0, c, nt - 1 - c)
    tc, bw = u_ref.shape
    h = SUBLANES
    row0 = pl.multiple_of(cc * tc, tc)
    lane_tiles = [slice(l * LANES, (l + 1) * LANES) for l in range(bw // LANES)]

    @pl.when(c == 0)
    def _():
        st_ref[...] = jnp.zeros_like(st_ref)

    def conv():
        pad_ref[0:h] = jnp.where(cc > 0, up_ref[...], 0.0)
        pad_ref[h:h + tc] = u_ref[...]
        pad_ref[h + tc:2 * h + tc] = jnp.where(cc < nt - 1, un_ref[...], 0.0)
        cw = cw_ref[...]
        uc = cb_ref[...]
        for t in range(CONV_WIDTH):
            off = h + t - CONV_LEFT
            uc = uc + pad_ref[off:off + tc] * cw[t:t + 1]
        return uc

    def sweep(uc, wa, ba, wi, bi, lam, reverse):
        ub = uc.astype(BF16)

        def gate(w_ref, b_ref):
            z = jnp.dot(ub, w_ref[...].astype(BF16), preferred_element_type=F32) + b_ref[...]
            return _sigmoid(z)

        r = gate(wa, ba)
        ig = gate(wi, bi)
        decay = (LRU_C * _softplus(-lam[...])) * r
        a = jnp.exp(-decay)
        y = jnp.tanh(decay) * (1.0 + a * a)
        root = jnp.where(y > 0.0, y * lax.rsqrt(y), 0.0)
        x = root * (ig * uc)
        for l, lanes in enumerate(lane_tiles):
            a_ref[l] = a[:, lanes]
            x_ref[l] = x[:, lanes]
        for l, lanes in enumerate(lane_tiles):
            st_ref[0:1, lanes] = _blocked_scan(
                a_ref.at[l], x_ref.at[l], h_ref.at[l], gp_ref.at[l], gh_ref.at[l], ent_ref.at[l],
                l3_ref.at[l], d3_ref.at[l], e3_ref.at[l], st_ref[0:1, lanes], reverse)

    @pl.when(ph == 0)
    def _():
        uc = conv()
        uc_ref[pl.ds(row0, tc), :] = uc
        sweep(uc, waf_ref, baf_ref, wif_ref, bif_ref, lamf_ref, False)
        for l in range(len(lane_tiles)):
            hf_ref[l, pl.ds(row0, tc), :] = h_ref[l]

    @pl.when(ph == 1)
    def _():
        sweep(uc_ref[pl.ds(row0, tc), :], wab_ref, bab_ref, wib_ref, bib_ref, lamb_ref, True)
        for l, lanes in enumerate(lane_tiles):
            hsum = hf_ref[l, pl.ds(row0, tc), :] + h_ref[l]
            y_ref[:, lanes] = (hsum * _gelu_tanh(gate_ref[:, lanes])).astype(y_ref.dtype)


def rglru(ug, layer, conv_w, conv_b, dirs, *, B, S, W, tc=1024):
    T = B * S
    bw = LRU_BLOCK_W
    nb = W // bw
    tc = min(tc, S)
    nt = S // tc
    h = SUBLANES
    assert tc % (h * h) == 0 and S % tc == 0
    last8 = T // h - 1

    def early(p, c):
        return c + p * (nt - 1 - c)

    def late(p, c):
        return nt - 1 - p * c

    in_specs = [
        pl.BlockSpec((tc, bw), lambda n, b, p, c: (b * nt + early(p, c), n)),
        pl.BlockSpec((h, bw), lambda n, b, p, c: (
            jnp.maximum((b * S + early(p, c) * tc) // h - 1, 0), n)),
        pl.BlockSpec((h, bw), lambda n, b, p, c: (
            jnp.minimum((b * S + (early(p, c) + 1) * tc) // h, last8), n)),
        pl.BlockSpec((tc, bw), lambda n, b, p, c: (b * nt + late(p, c), nb + n)),
        pl.BlockSpec((None, CONV_WIDTH, bw), lambda n, b, p, c: (layer, 0, n)),
        pl.BlockSpec((None, 1, bw), lambda n, b, p, c: (layer, 0, n))]
    args = [ug, ug, ug, ug, conv_w, conv_b.reshape(conv_b.shape[0], 1, W)]
    wspec = pl.BlockSpec((None, None, bw, bw), lambda n, b, p, c: (layer, n, 0, 0))
    vspec = pl.BlockSpec((None, None, 1, bw), lambda n, b, p, c: (layer, n, 0, 0))
    for wa, ba, wi, bi, lam in dirs:
        L = wa.shape[0]
        in_specs += [wspec, vspec, wspec, vspec, vspec]
        args += [wa, ba.reshape(L, nb, 1, bw), wi, bi.reshape(L, nb, 1, bw),
                 lam.reshape(L, nb, 1, bw)]
    g = tc // h
    nl = bw // LANES

    def tiles(rows):
        return pltpu.VMEM((nl, rows, LANES), F32)

    return pl.pallas_call(
        _lru_kernel,
        grid=(nb, B, 2, nt),
        in_specs=in_specs,
        out_specs=pl.BlockSpec((tc, bw), lambda n, b, p, c: (b * nt + late(p, c), n)),
        out_shape=jax.ShapeDtypeStruct((T, W), BF16),
        scratch_shapes=[pltpu.VMEM((tc + 2 * h, bw), F32),
                        tiles(tc),
                        tiles(tc),
                        tiles(tc),
                        tiles(g),
                        tiles(g),
                        tiles(g),
                        tiles(g // h),
                        tiles(g // h),
                        tiles(g // h),
                        tiles(S),
                        pltpu.VMEM((S, bw), F32),
                        pltpu.VMEM((h, bw), F32)],
        compiler_params=_params("parallel", "parallel", "arbitrary", "arbitrary"),
        name="rglru",
    )(*args)


def _lambda_init(layer_idx):
    return 0.8 - 0.6 * math.exp(-0.3 * layer_idx)


def kernel(x, mem, positions, attn_norm_g, attn_w_qkv, attn_lambda_q1, attn_lambda_k1, attn_lambda_q2, attn_lambda_k2, attn_subln_g, attn_w_o, rnn_norm_g, rnn_w_in, rnn_conv_w, rnn_conv_b, rnn_wa_f, rnn_ba_f, rnn_wi_f, rnn_bi_f, rnn_lam_f, rnn_wa_b, rnn_ba_b, rnn_wi_b, rnn_bi_b, rnn_lam_b, rnn_w_out, xattn_norm_g, xattn_mem_g, xattn_w_q, xattn_w_kv, xattn_w_o, mlp_norm_g, mlp_w1, mlp_w2, final_g):
    B, S, D = x.shape
    T = B * S
    depth = xattn_norm_g.shape[0]
    M = mem.shape[1]
    W = rnn_lam_f.shape[-1]

    h = x.reshape(T, D)
    mem2 = mem.reshape(B * M, D)
    rope = rope_tables(positions)
    lam4 = jnp.stack([attn_lambda_q1, attn_lambda_k1, attn_lambda_q2, attn_lambda_k2], axis=1)

    stats = None

    def normed(w3, layer, g2, **kw):
        if stats is None:
            return matmul(rmsnorm(h, g2, layer, BF16), w3, layer, **kw)
        return matmul(stats[0], w3, layer, norm=(g2[layer], stats[1]), **kw)

    nslab = mlp_w2.shape[1] // min(MM_TK, mlp_w2.shape[1])
    for i in range(depth):
        j = i // N_MIXERS
        if i % N_MIXERS == 0:
            qkv = normed(attn_w_qkv, j, attn_norm_g, out_dtype=BF16, tm=512, tn=1024,
                         scale=DA_HEAD_DIM ** -0.5, rope=rope, rope_width=D,
                         vmem_limit_bytes=QKV_VMEM_LIMIT_BYTES)
            o = diff_attention(qkv, lam4, attn_subln_g, j, B=B, S=S, D=D,
                               lam_init=_lambda_init(i))
            h, *stats = matmul(o, attn_w_o, j, out_dtype=F32, res=h, stats_out=True)
        else:
            ug = normed(rnn_w_in, j, rnn_norm_g, out_dtype=F32)
            y = rglru(ug, j, rnn_conv_w, rnn_conv_b,
                      ((rnn_wa_f, rnn_ba_f, rnn_wi_f, rnn_bi_f, rnn_lam_f),
                       (rnn_wa_b, rnn_ba_b, rnn_wi_b, rnn_bi_b, rnn_lam_b)),
                      B=B, S=S, W=W)
            h, *stats = matmul(y, rnn_w_out, j, out_dtype=F32, res=h, stats_out=True)

        q = normed(xattn_w_q, i, xattn_norm_g, out_dtype=BF16, scale=X_HEAD_DIM ** -0.5)
        kv = matmul(rmsnorm(mem2, xattn_mem_g, i, BF16), xattn_w_kv, i, out_dtype=BF16)
        o = xattn_core(q, kv, B=B, S=S)
        h, *stats = matmul(o, xattn_w_o, i, out_dtype=F32, res=h, tm=512, tn=2048,
                           stats_out=True)

        hid = normed(mlp_w1, i, mlp_norm_g, out_dtype=BF16, relu2=True)
        for ks in range(nslab):
            if ks == nslab - 1 and i < depth - 1:
                h, *stats = matmul(hid, mlp_w2, i, out_dtype=F32, res=h, kslab=ks, stats_out=True)
            else:
                h = matmul(hid, mlp_w2, i, out_dtype=F32, res=h, kslab=ks)

    return rmsnorm(h, final_g.reshape(1, D), 0, F32).reshape(B, S, D)
```
